```python
import math
import jax, jax.numpy as jnp
from jax import lax
import numpy as np

D_MODEL = 2048
BATCH = 8
SEQ = 2048
DEPTH = 1

D_ATT = D_MODEL // 2
ATT_HEAD_DIM = 128
N_ATT_HEADS = D_ATT // ATT_HEAD_DIM
Q_BLOCK = 128
D_SSD = D_MODEL
SSD_HEAD_DIM = 64
N_SSD_HEADS = D_SSD // SSD_HEAD_DIM
SSD_GROUPS = 8
D_STATE = 128
CONV_K = 4
CHUNK = 128
D_XBC = D_SSD + 2 * SSD_GROUPS * D_STATE
D_MIX = D_ATT + D_SSD
PROJ_SIZES = (D_ATT, D_ATT, D_ATT, N_ATT_HEADS, D_ATT, D_SSD, D_XBC, N_SSD_HEADS)
D_PROJ = sum(PROJ_SIZES)
NORM_EPS = 1e-5

kernel_name = "fox_ssd_parallel_heads_deepnorm"


def _split_points(sizes):
    pts, acc = [], 0
    for s in sizes[:-1]:
        acc += s
        pts.append(acc)
    return pts


def layer_norm(v, g, b):
    vf = v.astype(jnp.float32)
    mu = jnp.mean(vf, axis=-1, keepdims=True)
    var = jnp.mean(jnp.square(vf - mu), axis=-1, keepdims=True)
    out = (vf - mu) * lax.rsqrt(var + NORM_EPS) * g.astype(jnp.float32) + b.astype(jnp.float32)
    return out.astype(v.dtype)


def fox_attention(q, k, v, log_f):
    b, l, h, d = q.shape
    nb = l // Q_BLOCK
    scale = 1.0 / math.sqrt(d)
    F = jnp.cumsum(log_f.astype(jnp.float32), axis=1)
    Fk = F.transpose(0, 2, 1)
    kf = k.astype(jnp.float32)
    vf = v.astype(jnp.float32)
    qb = q.astype(jnp.float32).reshape(b, nb, Q_BLOCK, h, d).transpose(1, 0, 2, 3, 4)
    Fq = Fk.reshape(b, h, nb, Q_BLOCK).transpose(2, 0, 1, 3)
    q_pos = jnp.arange(l).reshape(nb, Q_BLOCK)
    k_pos = jnp.arange(l)

    def block(args):
        qi, Fi, pos = args
        s = jnp.einsum('bqhd,bkhd->bhqk', qi, kf) * scale
        s = s + (Fi[..., :, None] - Fk[..., None, :])
        s = jnp.where(pos[:, None] >= k_pos[None, :], s, -jnp.inf)
        p = jax.nn.softmax(s, axis=-1)
        return jnp.einsum('bhqk,bkhd->bqhd', p, vf)

    out = lax.map(block, (qb, Fq, q_pos))
    return out.transpose(1, 0, 2, 3, 4).reshape(b, l, h * d)


def causal_depthwise_conv(u, w, bias):
    c = u.shape[-1]
    out = lax.conv_general_dilated(
        u, w[:, None, :].astype(u.dtype), window_strides=(1,), padding=[(CONV_K - 1, 0)],
        dimension_numbers=('NWC', 'WIO', 'NWC'), feature_group_count=c)
    return out + bias


def ssd_chunked(xs, dt, A, Bm, Cm):
    b, l, H, P = xs.shape
    G, N = Bm.shape[2], Bm.shape[3]
    r = H // G
    c = l // CHUNK
    X = xs.astype(jnp.float32).reshape(b, c, CHUNK, G, r, P)
    dtc = dt.reshape(b, c, CHUNK, G, r)
    Bc = Bm.astype(jnp.float32).reshape(b, c, CHUNK, G, N)
    Cc = Cm.astype(jnp.float32).reshape(b, c, CHUNK, G, N)
    a_cs = jnp.cumsum(dtc * A.reshape(G, r), axis=2)
    Xdt = X * dtc[..., None]
    a_t = a_cs.transpose(0, 1, 3, 4, 2)
    seg = a_t[..., :, None] - a_t[..., None, :]
    causal = jnp.tril(jnp.ones((CHUNK, CHUNK), dtype=bool))
    decay = jnp.exp(jnp.where(causal, seg, -jnp.inf))
    cb = jnp.einsum('bclgn,bcsgn->bcgls', Cc, Bc)
    m = cb[:, :, :, None] * decay
    y_diag = jnp.einsum('bcgrls,bcsgrp->bclgrp', m, Xdt)
    decay_end = jnp.exp(a_cs[:, :, -1:] - a_cs)
    states = jnp.einsum('bclgn,bclgr,bclgrp->bcgrpn', Bc, decay_end, Xdt)
    chunk_decay = jnp.exp(a_cs[:, :, -1])

    def step(carry, inp):
        st, dec = inp
        return carry * dec[..., None, None] + st, carry

    init = jnp.zeros((b, G, r, P, N), jnp.float32)
    _, prev = lax.scan(step, init, (states.transpose(1, 0, 2, 3, 4, 5), chunk_decay.transpose(1, 0, 2, 3)))
    prev = prev.transpose(1, 0, 2, 3, 4, 5)
    y_off = jnp.einsum('bclgn,bcgrpn,bclgr->bclgrp', Cc, prev, jnp.exp(a_cs))
    return (y_diag + y_off).reshape(b, l, H, P)


def gated_group_rmsnorm(y, z, w):
    b, l, d = y.shape
    u = (y.astype(jnp.float32) * jax.nn.silu(z.astype(jnp.float32))).reshape(b, l, SSD_GROUPS, d // SSD_GROUPS)
    u = u * lax.rsqrt(jnp.mean(jnp.square(u), axis=-1, keepdims=True) + NORM_EPS)
    return u.reshape(b, l, d) * w.astype(jnp.float32)


def hybrid_mixer(x, w_in, b_forget, conv_w, conv_b, dt_bias, a_log, d_skip, ssd_norm_w, w_out):
    b, l, _ = x.shape
    proj = x @ w_in
    q, k, v, f_logit, z_att, z_ssd, xbc, dt_raw = jnp.split(proj, _split_points(PROJ_SIZES), axis=-1)
    log_f = jax.nn.log_sigmoid(f_logit.astype(jnp.float32) + b_forget.astype(jnp.float32))
    att = fox_attention(q.reshape(b, l, N_ATT_HEADS, ATT_HEAD_DIM),
                        k.reshape(b, l, N_ATT_HEADS, ATT_HEAD_DIM),
                        v.reshape(b, l, N_ATT_HEADS, ATT_HEAD_DIM), log_f)
    att = att * jax.nn.silu(z_att.astype(jnp.float32))
    xbc = jax.nn.silu(causal_depthwise_conv(xbc, conv_w, conv_b))
    xs, Bm, Cm = jnp.split(xbc, [D_SSD, D_SSD + SSD_GROUPS * D_STATE], axis=-1)
    xs = xs.reshape(b, l, N_SSD_HEADS, SSD_HEAD_DIM)
    dt = jax.nn.softplus(dt_raw.astype(jnp.float32) + dt_bias.astype(jnp.float32))
    A = -jnp.exp(a_log.astype(jnp.float32))
    y = ssd_chunked(xs, dt, A, Bm.reshape(b, l, SSD_GROUPS, D_STATE), Cm.reshape(b, l, SSD_GROUPS, D_STATE))
    y = y + d_skip.astype(jnp.float32)[:, None] * xs.astype(jnp.float32)
    y = gated_group_rmsnorm(y.reshape(b, l, D_SSD), z_ssd, ssd_norm_w)
    mixed = jnp.concatenate([att, y], axis=-1).astype(x.dtype)
    return mixed @ w_out


def setup_inputs(seed: int = 0) -> dict:
    key = jax.random.key(seed)
    ks = jax.random.split(key, 12)
    beta = (8.0 * DEPTH) ** -0.25
    x = jax.random.normal(ks[0], (BATCH, SEQ, D_MODEL), jnp.float32)
    w_in = jax.random.normal(ks[1], (DEPTH, D_MODEL, D_PROJ), jnp.float32) * D_MODEL ** -0.5
    b_forget = 2.0 + 0.5 * jax.random.normal(ks[2], (DEPTH, N_ATT_HEADS), jnp.float32)
    conv_w = jax.random.normal(ks[3], (DEPTH, CONV_K, D_XBC), jnp.float32) * CONV_K ** -0.5
    conv_b = 0.02 * jax.random.normal(ks[4], (DEPTH, D_XBC), jnp.float32)
    dt0 = jnp.exp(jax.random.uniform(ks[5], (DEPTH, N_SSD_HEADS), jnp.float32,
                                     math.log(1e-3), math.log(1e-1)))
    dt_bias = dt0 + jnp.log(-jnp.expm1(-dt0))
    a_log = jnp.log(jax.random.uniform(ks[6], (DEPTH, N_SSD_HEADS), jnp.float32, 1.0, 16.0))
    d_skip = 1.0 + 0.1 * jax.random.normal(ks[7], (DEPTH, N_SSD_HEADS), jnp.float32)
    ssd_norm_w = 1.0 + 0.1 * jax.random.normal(ks[8], (DEPTH, D_SSD), jnp.float32)
    w_out = jax.random.normal(ks[9], (DEPTH, D_MIX, D_MODEL), jnp.float32) * (D_MIX ** -0.5) * beta
    ln_g = 1.0 + 0.1 * jax.random.normal(ks[10], (DEPTH, D_MODEL), jnp.float32)
    ln_b = 0.02 * jax.random.normal(ks[11], (DEPTH, D_MODEL), jnp.float32)
    return {"x": x, "w_in": w_in, "b_forget": b_forget, "conv_w": conv_w, "conv_b": conv_b,
            "dt_bias": dt_bias, "a_log": a_log, "d_skip": d_skip, "ssd_norm_w": ssd_norm_w,
            "w_out": w_out, "ln_g": ln_g, "ln_b": ln_b}


def reference(x, w_in, b_forget, conv_w, conv_b, dt_bias, a_log, d_skip, ssd_norm_w, w_out, ln_g, ln_b):
    alpha = (2.0 * DEPTH) ** 0.25
    for i in range(DEPTH):
        h = hybrid_mixer(x, w_in[i], b_forget[i], conv_w[i], conv_b[i], dt_bias[i], a_log[i],
                         d_skip[i], ssd_norm_w[i], w_out[i])
        x = layer_norm(alpha * x + h.astype(x.dtype), ln_g[i], ln_b[i])
    return x
```

```python
import functools
import math

import jax
import jax.numpy as jnp
from jax import lax
from jax.experimental import pallas as pl
from jax.experimental.pallas import tpu as pltpu

F32 = jnp.float32
BF16 = jnp.bfloat16

D_MODEL = 2048
D_ATT = 1024
HEAD_DIM = 128
N_ATT_HEADS = 8
D_SSD = 2048
SSD_HEAD_DIM = 64
N_SSD_HEADS = 32
SSD_GROUPS = 8
HEADS_PER_GROUP = N_SSD_HEADS // SSD_GROUPS
D_STATE = 128
CONV_K = 4
CHUNK = 128
D_BC = SSD_GROUPS * D_STATE
D_XBC = D_SSD + 2 * D_BC
D_MIX = D_ATT + D_SSD
NORM_EPS = 1e-5
LANES = 128
GROUP_W = HEADS_PER_GROUP * SSD_HEAD_DIM

_OFF_F = 3 * D_ATT
_OFF_ZATT = _OFF_F + N_ATT_HEADS
_OFF_DT = _OFF_ZATT + D_ATT + D_SSD + D_XBC
D_MAIN = 3 * D_ATT + D_ATT + D_SSD + D_XBC
F_LANE0 = 0
DT_LANE0 = N_ATT_HEADS

VMEM_LIMIT = 56 * 1024 * 1024


def _cparams(sem, **kw):
    return pltpu.CompilerParams(dimension_semantics=sem, vmem_limit_bytes=VMEM_LIMIT, **kw)


TM_IN = 1024
TN_IN = 1024


def _inproj_kernel(x_ref, w_ref, ws_ref, proj_ref, small_ref, xb_ref):
    @pl.when(pl.program_id(1) == 0)
    def _():
        xb = x_ref[...].astype(BF16)
        xb_ref[...] = xb
        small_ref[...] = jnp.dot(xb, ws_ref[...], preferred_element_type=F32)

    proj_ref[...] = jnp.dot(xb_ref[...], w_ref[...], preferred_element_type=F32).astype(BF16)


def _inproj(x2, w_main, w_small):
    m = x2.shape[0]
    return pl.pallas_call(
        _inproj_kernel,
        grid=(m // TM_IN, D_MAIN // TN_IN),
        in_specs=[
            pl.BlockSpec((TM_IN, D_MODEL), lambda i, j: (i, 0)),
            pl.BlockSpec((D_MODEL, TN_IN), lambda i, j: (0, j)),
            pl.BlockSpec((D_MODEL, LANES), lambda i, j: (0, 0)),
        ],
        out_specs=[
            pl.BlockSpec((TM_IN, TN_IN), lambda i, j: (i, j)),
            pl.BlockSpec((TM_IN, LANES), lambda i, j: (i, 0)),
        ],
        out_shape=[
            jax.ShapeDtypeStruct((m, D_MAIN), BF16),
            jax.ShapeDtypeStruct((m, LANES), F32),
        ],
        scratch_shapes=[pltpu.VMEM((TM_IN, D_MODEL), BF16)],
        compiler_params=_cparams(("parallel", "arbitrary")),
        name="inproj",
    )(x2, w_main, w_small)


def _split_dot(t_bf16, val):
    hi = val.astype(BF16)
    r1 = val - hi.astype(F32)
    mid = r1.astype(BF16)
    lo = (r1 - mid.astype(F32)).astype(BF16)
    return (jnp.dot(t_bf16, hi, preferred_element_type=F32)
            + jnp.dot(t_bf16, mid, preferred_element_type=F32)
            + jnp.dot(t_bf16, lo, preferred_element_type=F32))


def _gate_kernel(small_ref, bias_ref, alog_ref, col_ref, dt_ref, row_ref):
    seq = small_ref.shape[0]
    lane = lax.broadcasted_iota(jnp.int32, (CHUNK, LANES), 1)
    is_f = lane < DT_LANE0
    is_dt = jnp.logical_and(lane >= DT_LANE0, lane < DT_LANE0 + N_SSD_HEADS)
    r = lax.broadcasted_iota(jnp.int32, (CHUNK, CHUNK), 0)
    c = lax.broadcasted_iota(jnp.int32, (CHUNK, CHUNK), 1)
    tri = jnp.where(r >= c, 1.0, 0.0).astype(BF16)
    a_neg = -jnp.exp(alog_ref[...])
    bias = bias_ref[...]
    carry = jnp.zeros((1, LANES), F32)
    for ci in range(seq // CHUNK):
        rows = slice(ci * CHUNK, (ci + 1) * CHUNK)
        v = small_ref[rows, :] + bias
        t = jnp.log1p(jnp.exp(-jnp.abs(v)))
        log_f = jnp.minimum(v, 0.0) - t
        dt = jnp.maximum(v, 0.0) + t
        val = jnp.where(is_f, log_f, jnp.where(is_dt, dt * a_neg, 0.0))
        out = _split_dot(tri, val) + carry
        col_ref[rows, :] = out
        dt_ref[rows, :] = jnp.where(is_dt, dt, 0.0)
        row_ref[ci] = out.T
        carry = jnp.where(is_f[:1], out[CHUNK - 1:CHUNK, :], 0.0)


def _gates(small3, bias_vec, alog_vec):
    b, seq, _ = small3.shape
    nc = seq // CHUNK
    return pl.pallas_call(
        _gate_kernel,
        grid=(b,),
        in_specs=[
            pl.BlockSpec((None, seq, LANES), lambda i: (i, 0, 0)),
            pl.BlockSpec((1, LANES), lambda i: (0, 0)),
            pl.BlockSpec((1, LANES), lambda i: (0, 0)),
        ],
        out_specs=[
            pl.BlockSpec((None, seq, LANES), lambda i: (i, 0, 0)),
            pl.BlockSpec((None, seq, LANES), lambda i: (i, 0, 0)),
            pl.BlockSpec((None, nc, CHUNK, LANES), lambda i: (i, 0, 0, 0)),
        ],
        out_shape=[
            jax.ShapeDtypeStruct((b, seq, LANES), F32),
            jax.ShapeDtypeStruct((b, seq, LANES), F32),
            jax.ShapeDtypeStruct((b, nc, CHUNK, LANES), F32),
        ],
        compiler_params=_cparams(("parallel",)),
        name="gates",
    )(small3, bias_vec, alog_vec)


TQ = 512
TK = 512
ATT_SCALE = 1.0 / math.sqrt(HEAD_DIM)


def _attn_kernel(q_ref, k_ref, v_ref, z_ref, fc_ref, fr_ref, o_ref):
    i = pl.program_id(1)
    rr = lax.broadcasted_iota(jnp.int32, (TQ, TK), 0)
    cc = lax.broadcasted_iota(jnp.int32, (TQ, TK), 1)
    causal = rr >= cc
    for h in range(N_ATT_HEADS):
        hs = slice(h * HEAD_DIM, (h + 1) * HEAD_DIM)
        q = q_ref[:, hs]
        fq = fc_ref[:, F_LANE0 + h:F_LANE0 + h + 1]

        def step(j, carry, masked):
            m, l, acc = carry
            ks = pl.ds(pl.multiple_of(j * TK, TK), TK)
            k = k_ref[ks, hs]
            v = v_ref[ks, hs]
            fk = fr_ref[j, h:h + 1, :]
            s = lax.dot_general(q, k, (((1,), (1,)), ((), ())), preferred_element_type=F32)
            s = s * ATT_SCALE + (fq - fk)
            if masked:
                s = jnp.where(causal, s, -jnp.inf)
            m_new = jnp.maximum(m, jnp.max(s, axis=1, keepdims=True))
            p = jnp.exp(s - m_new)
            a = jnp.exp(m - m_new)
            l = a * l + jnp.sum(p, axis=1, keepdims=True)
            acc = a * acc + jnp.dot(p.astype(BF16), v, preferred_element_type=F32)
            return m_new, l, acc

        init = (jnp.full((TQ, 1), -jnp.inf, F32), jnp.zeros((TQ, 1), F32), jnp.zeros((TQ, HEAD_DIM), F32))
        carry = lax.fori_loop(0, i, functools.partial(step, masked=False), init)
        _, l, acc = step(i, carry, True)
        z = z_ref[:, hs].astype(F32)
        o_ref[:, hs] = ((acc / l) * (z * jax.nn.sigmoid(z))).astype(BF16)


def _attention(proj3, fcol, frow):
    b, seq, _ = proj3.shape
    return pl.pallas_call(
        _attn_kernel,
        grid=(b, seq // TQ),
        in_specs=[
            pl.BlockSpec((None, TQ, D_ATT), lambda bi, i: (bi, i, 0)),
            pl.BlockSpec((None, seq, D_ATT), lambda bi, i: (bi, 0, 1)),
            pl.BlockSpec((None, seq, D_ATT), lambda bi, i: (bi, 0, 2)),
            pl.BlockSpec((None, TQ, D_ATT), lambda bi, i: (bi, i, 3)),
            pl.BlockSpec((None, TQ, LANES), lambda bi, i: (bi, i, 0)),
            pl.BlockSpec((None, seq // TK, N_ATT_HEADS, TK), lambda bi, i: (bi, 0, 0, 0)),
        ],
        out_specs=pl.BlockSpec((None, TQ, D_ATT), lambda bi, i: (bi, i, 0)),
        out_shape=jax.ShapeDtypeStruct((b, seq, D_ATT), BF16),
        compiler_params=_cparams(("parallel", "arbitrary")),
        name="fox_attention",
    )(proj3, proj3, proj3, proj3, fcol, frow)


CONV_PAD = 8
CONV_COLS = 512


def _split2_dot(val, e_bf16):
    hi = val.astype(BF16)
    lo = (val - hi.astype(F32)).astype(BF16)
    return (jnp.dot(hi, e_bf16, preferred_element_type=F32)
            + jnp.dot(lo, e_bf16, preferred_element_type=F32))


def _ssd_kernel(z_ref, xr_ref, bcr_ref, cw_ref, cb_ref, acol_ref, arow_ref, dt_ref, e_ref, dsk_ref, nw_ref,
                y_ref, buf_ref, st_ref, xs_ref, bc_ref, ex_ref, xdt_ref, xdec_ref, yacc_ref):
    c = pl.program_id(1)

    @pl.when(c == 0)
    def _():
        buf_ref[0:CONV_PAD, :] = jnp.zeros((CONV_PAD, D_XBC), F32)
        st_ref[...] = jnp.zeros_like(st_ref)

    buf_ref[CONV_PAD:CONV_PAD + CHUNK, 0:D_SSD] = xr_ref[...].astype(F32)
    buf_ref[CONV_PAD:CONV_PAD + CHUNK, D_SSD:D_XBC] = bcr_ref[...].astype(F32)
    for s in range(D_XBC // CONV_COLS):
        cs = slice(s * CONV_COLS, (s + 1) * CONV_COLS)
        u = cb_ref[:, cs]
        for kk in range(CONV_K):
            off = CONV_PAD - (CONV_K - 1) + kk
            u = u + cw_ref[kk:kk + 1, cs] * buf_ref[off:off + CHUNK, cs]
        u = u * jax.nn.sigmoid(u)
        if s * CONV_COLS < D_SSD:
            xs_ref[:, cs] = u
        else:
            bc_ref[:, s * CONV_COLS - D_SSD:(s + 1) * CONV_COLS - D_SSD] = u.astype(BF16)
    buf_ref[0:CONV_PAD, :] = buf_ref[CHUNK:CHUNK + CONV_PAD, :]

    lane = lax.broadcasted_iota(jnp.int32, (CHUNK, LANES), 1)
    is_dt = jnp.logical_and(lane >= DT_LANE0, lane < DT_LANE0 + N_SSD_HEADS)
    acs = acol_ref[...]
    a_last = acs[CHUNK - 1:CHUNK, :]
    exp_a = jnp.where(is_dt, jnp.exp(acs), 0.0)
    dec_end = jnp.where(is_dt, jnp.exp(a_last - acs), 0.0)
    e = e_ref[...]
    ex_ref[0] = _split2_dot(dt_ref[...], e)
    ex_ref[1] = _split2_dot(exp_a, e)
    ex_ref[2] = _split2_dot(dec_end, e)

    xdt = xs_ref[...] * ex_ref[0]
    xdt_ref[...] = xdt.astype(BF16)
    xdec_ref[...] = (xdt * ex_ref[2]).astype(BF16)

    rr = lax.broadcasted_iota(jnp.int32, (CHUNK, CHUNK), 0)
    cc = lax.broadcasted_iota(jnp.int32, (CHUNK, CHUNK), 1)
    causal = rr >= cc
    arow = arow_ref[...]
    for g in range(SSD_GROUPS):
        gs = slice(g * GROUP_W, (g + 1) * GROUP_W)
        bg = bc_ref[:, g * D_STATE:(g + 1) * D_STATE]
        cg = bc_ref[:, D_BC + g * D_STATE:D_BC + (g + 1) * D_STATE]
        cb = lax.dot_general(cg, bg, (((1,), (1,)), ((), ())), preferred_element_type=F32)
        st = st_ref[g]
        y_off = jnp.dot(cg, st.astype(BF16), preferred_element_type=F32) * ex_ref[1, :, gs]
        yacc_ref[:, gs] = y_off
        for rh in range(HEADS_PER_GROUP):
            h = g * HEADS_PER_GROUP + rh
            hsl = slice(h * SSD_HEAD_DIM, (h + 1) * SSD_HEAD_DIM)
            col = acs[:, DT_LANE0 + h:DT_LANE0 + h + 1]
            row = arow[DT_LANE0 + h:DT_LANE0 + h + 1, :]
            decay = jnp.exp(jnp.where(causal, col - row, -jnp.inf))
            mm = (cb * decay).astype(BF16)
            yacc_ref[:, hsl] += jnp.dot(mm, xdt_ref[:, hsl], preferred_element_type=F32)
        upd = lax.dot_general(bg, xdec_ref[:, gs], (((0,), (0,)), ((), ())), preferred_element_type=F32)
        st_ref[g] = st * ex_ref[1, CHUNK - 1:CHUNK, gs] + upd

    for g in range(SSD_GROUPS):
        gs = slice(g * GROUP_W, (g + 1) * GROUP_W)
        z = z_ref[:, gs].astype(F32)
        u = (yacc_ref[:, gs] + dsk_ref[:, gs] * xs_ref[:, gs]) * (z * jax.nn.sigmoid(z))
        ms = jnp.mean(u * u, axis=-1, keepdims=True)
        y_ref[:, gs] = (u * lax.rsqrt(ms + NORM_EPS) * nw_ref[:, gs]).astype(BF16)


def _ssd(proj3, conv_w, conv_b, acol, arow, dtv, expand, dskip_e, norm_w):
    b, seq, _ = proj3.shape
    nc = seq // CHUNK
    zb = (3 * D_ATT + D_ATT) // D_SSD
    return pl.pallas_call(
        _ssd_kernel,
        grid=(b, nc),
        in_specs=[
            pl.BlockSpec((None, CHUNK, D_SSD), lambda bi, ci: (bi, ci, zb)),
            pl.BlockSpec((None, CHUNK, D_SSD), lambda bi, ci: (bi, ci, zb + 1)),
            pl.BlockSpec((None, CHUNK, D_SSD), lambda bi, ci: (bi, ci, zb + 2)),
            pl.BlockSpec((CONV_K, D_XBC), lambda bi, ci: (0, 0)),
            pl.BlockSpec((1, D_XBC), lambda bi, ci: (0, 0)),
            pl.BlockSpec((None, CHUNK, LANES), lambda bi, ci: (bi, ci, 0)),
            pl.BlockSpec((None, None, CHUNK, LANES), lambda bi, ci: (bi, ci, 0, 0)),
            pl.BlockSpec((None, CHUNK, LANES), lambda bi, ci: (bi, ci, 0)),
            pl.BlockSpec((LANES, D_SSD), lambda bi, ci: (0, 0)),
            pl.BlockSpec((1, D_SSD), lambda bi, ci: (0, 0)),
            pl.BlockSpec((1, D_SSD), lambda bi, ci: (0, 0)),
        ],
        out_specs=pl.BlockSpec((None, CHUNK, D_SSD), lambda bi, ci: (bi, ci, 0)),
        out_shape=jax.ShapeDtypeStruct((b, seq, D_SSD), BF16),
        scratch_shapes=[
            pltpu.VMEM((CONV_PAD + CHUNK, D_XBC), F32),
            pltpu.VMEM((SSD_GROUPS, D_STATE, GROUP_W), F32),
            pltpu.VMEM((CHUNK, D_SSD), F32),
            pltpu.VMEM((CHUNK, 2 * D_BC), BF16),
            pltpu.VMEM((3, CHUNK, D_SSD), F32),
            pltpu.VMEM((CHUNK, D_SSD), BF16),
            pltpu.VMEM((CHUNK, D_SSD), BF16),
            pltpu.VMEM((CHUNK, D_SSD), F32),
        ],
        compiler_params=_cparams(("parallel", "arbitrary")),
        name="ssd",
    )(proj3, proj3, proj3, conv_w, conv_b, acol, arow, dtv, expand, dskip_e, norm_w)


TM_OUT = 256


def _out_kernel(att_ref, y_ref, w_ref, x_ref, g_ref, b_ref, o_ref, *, alpha):
    h = jnp.dot(att_ref[...], w_ref[0:D_ATT, :], preferred_element_type=F32)
    h = h + jnp.dot(y_ref[...], w_ref[D_ATT:D_MIX, :], preferred_element_type=F32)
    r = alpha * x_ref[...] + h
    mu = jnp.mean(r, axis=-1, keepdims=True)
    d = r - mu
    var = jnp.mean(d * d, axis=-1, keepdims=True)
    o_ref[...] = d * lax.rsqrt(var + NORM_EPS) * g_ref[...] + b_ref[...]


def _outproj(att2, y2, w_out, x2, ln_g, ln_b, alpha):
    m = x2.shape[0]
    return pl.pallas_call(
        functools.partial(_out_kernel, alpha=alpha),
        grid=(m // TM_OUT,),
        in_specs=[
            pl.BlockSpec((TM_OUT, D_ATT), lambda i: (i, 0)),
            pl.BlockSpec((TM_OUT, D_SSD), lambda i: (i, 0)),
            pl.BlockSpec((D_MIX, D_MODEL), lambda i: (0, 0)),
            pl.BlockSpec((TM_OUT, D_MODEL), lambda i: (i, 0)),
            pl.BlockSpec((1, D_MODEL), lambda i: (0, 0)),
            pl.BlockSpec((1, D_MODEL), lambda i: (0, 0)),
        ],
        out_specs=pl.BlockSpec((TM_OUT, D_MODEL), lambda i: (i, 0)),
        out_shape=jax.ShapeDtypeStruct((m, D_MODEL), F32),
        compiler_params=_cparams(("parallel",)),
        name="outproj_deepnorm",
    )(att2, y2, w_out, x2, ln_g, ln_b)


def _pad_lanes(vec, lane0):
    out = jnp.zeros((1, LANES), F32)
    return out.at[0, lane0:lane0 + vec.shape[0]].set(vec.astype(F32))


def _layer(x, w_in, b_forget, conv_w, conv_b, dt_bias, a_log, d_skip, ssd_norm_w, w_out, ln_g, ln_b, alpha):
    b, seq, _ = x.shape
    x2 = x.reshape(b * seq, D_MODEL)

    w_main = jnp.concatenate([w_in[:, :_OFF_F], w_in[:, _OFF_ZATT:_OFF_DT]], axis=1).astype(BF16)
    w_small = jnp.concatenate(
        [w_in[:, _OFF_F:_OFF_ZATT], w_in[:, _OFF_DT:],
         jnp.zeros((D_MODEL, LANES - N_ATT_HEADS - N_SSD_HEADS), w_in.dtype)], axis=1).astype(BF16)
    bias_vec = _pad_lanes(b_forget, F_LANE0) + _pad_lanes(dt_bias, DT_LANE0)
    alog_vec = _pad_lanes(a_log, DT_LANE0)
    heads = jnp.arange(D_SSD, dtype=jnp.int32) // SSD_HEAD_DIM
    expand = (jnp.arange(LANES, dtype=jnp.int32)[:, None] == heads[None, :] + DT_LANE0).astype(BF16)
    dskip_e = jnp.repeat(d_skip.astype(F32), SSD_HEAD_DIM)[None, :]

    proj, small = _inproj(x2, w_main, w_small)
    proj3 = proj.reshape(b, seq, D_MAIN)
    gcol, dtv, grow = _gates(small.reshape(b, seq, LANES), bias_vec, alog_vec)

    nk = seq // TK
    frow = grow[:, :, :N_ATT_HEADS, :].reshape(b, nk, TK // CHUNK, N_ATT_HEADS, CHUNK)
    frow = frow.transpose(0, 1, 3, 2, 4).reshape(b, nk, N_ATT_HEADS, TK)

    att = _attention(proj3, gcol, frow)
    y = _ssd(proj3, conv_w.astype(F32), conv_b.astype(F32)[None, :], gcol, grow, dtv, expand, dskip_e,
             ssd_norm_w.astype(F32)[None, :])
    out = _outproj(att.reshape(b * seq, D_ATT), y.reshape(b * seq, D_SSD), w_out.astype(BF16), x2,
                   ln_g.astype(F32)[None, :], ln_b.astype(F32)[None, :], alpha)
    return out.reshape(b, seq, D_MODEL)


def kernel(x, w_in, b_forget, conv_w, conv_b, dt_bias, a_log, d_skip, ssd_norm_w, w_out, ln_g, ln_b):
    depth = w_in.shape[0]
    alpha = (2.0 * depth) ** 0.25
    for i in range(depth):
        x = _layer(x, w_in[i], b_forget[i], conv_w[i], conv_b[i], dt_bias[i], a_log[i], d_skip[i],
                   ssd_norm_w[i], w_out[i], ln_g[i], ln_b[i], alpha)
    return x
```

```python
import functools
import math

import jax
import jax.numpy as jnp
from jax import lax
from jax.experimental import pallas as pl
from jax.experimental.pallas import tpu as pltpu

F32 = jnp.float32
BF16 = jnp.bfloat16

D_MODEL = 2048
D_ATT = 1024
HEAD_DIM = 128
N_ATT_HEADS = 8
D_SSD = 2048
SSD_HEAD_DIM = 64
N_SSD_HEADS = 32
SSD_GROUPS = 8
HEADS_PER_GROUP = N_SSD_HEADS // SSD_GROUPS
D_STATE = 128
CONV_K = 4
CHUNK = 128
D_BC = SSD_GROUPS * D_STATE
D_XBC = D_SSD + 2 * D_BC
D_MIX = D_ATT + D_SSD
NORM_EPS = 1e-5
LANES = 128
GROUP_W = HEADS_PER_GROUP * SSD_HEAD_DIM

_OFF_F = 3 * D_ATT
_OFF_ZATT = _OFF_F + N_ATT_HEADS
_OFF_DT = _OFF_ZATT + D_ATT + D_SSD + D_XBC
D_MAIN = 3 * D_ATT + D_ATT + D_SSD + D_XBC
F_LANE0 = 0
DT_LANE0 = N_ATT_HEADS

AUG_PIECES = 3
AUG_Q0 = AUG_PIECES * N_ATT_HEADS
LOG2E = math.log2(math.e)

VMEM_LIMIT = 56 * 1024 * 1024


def _cparams(sem, **kw):
    return pltpu.CompilerParams(dimension_semantics=sem, vmem_limit_bytes=VMEM_LIMIT, **kw)


TM_IN = 1024
TN_IN = 1024


N_QKV_TILES = 3 * D_ATT // TN_IN


def _inproj_kernel(x_ref, wa_ref, wb_ref, ws_ref, proj_ref, small_ref, xb_ref):
    j = pl.program_id(1)

    @pl.when(j == 0)
    def _():
        xb = x_ref[...].astype(BF16)
        xb_ref[...] = xb
        small_ref[...] = jnp.dot(xb, ws_ref[...], preferred_element_type=F32)

    @pl.when(j < N_QKV_TILES)
    def _():
        proj_ref[...] = jnp.dot(xb_ref[...], wa_ref[...], preferred_element_type=F32).astype(BF16)

    @pl.when(j >= N_QKV_TILES)
    def _():
        proj_ref[...] = jnp.dot(xb_ref[...], wb_ref[...], preferred_element_type=F32).astype(BF16)


def _inproj(x2, w_qkv, w_rest, w_small):
    m = x2.shape[0]
    return pl.pallas_call(
        _inproj_kernel,
        grid=(m // TM_IN, D_MAIN // TN_IN),
        in_specs=[
            pl.BlockSpec((TM_IN, D_MODEL), lambda i, j: (i, 0)),
            pl.BlockSpec((D_MODEL, TN_IN), lambda i, j: (0, jnp.minimum(j, N_QKV_TILES - 1))),
            pl.BlockSpec((D_MODEL, TN_IN), lambda i, j: (0, jnp.maximum(j - N_QKV_TILES, 0))),
            pl.BlockSpec((D_MODEL, LANES), lambda i, j: (0, 0)),
        ],
        out_specs=[
            pl.BlockSpec((TM_IN, TN_IN), lambda i, j: (i, j)),
            pl.BlockSpec((TM_IN, LANES), lambda i, j: (i, 0)),
        ],
        out_shape=[
            jax.ShapeDtypeStruct((m, D_MAIN), BF16),
            jax.ShapeDtypeStruct((m, LANES), F32),
        ],
        scratch_shapes=[pltpu.VMEM((TM_IN, D_MODEL), BF16)],
        compiler_params=_cparams(("parallel", "arbitrary")),
        name="inproj",
    )(x2, w_qkv, w_rest, w_small)


def _split_dot(t_bf16, val):
    hi = val.astype(BF16)
    r1 = val - hi.astype(F32)
    mid = r1.astype(BF16)
    lo = (r1 - mid.astype(F32)).astype(BF16)
    return (jnp.dot(t_bf16, hi, preferred_element_type=F32)
            + jnp.dot(t_bf16, mid, preferred_element_type=F32)
            + jnp.dot(t_bf16, lo, preferred_element_type=F32))


def _split3(val):
    hi = val.astype(BF16)
    r1 = val - hi.astype(F32)
    mid = r1.astype(BF16)
    lo = (r1 - mid.astype(F32)).astype(BF16)
    return hi, mid, lo


def _gate_kernel(small_ref, bias_ref, alog_ref, col_ref, dt_ref, row_ref, augq_ref, augk_ref):
    seq = small_ref.shape[0]
    lane = lax.broadcasted_iota(jnp.int32, (CHUNK, LANES), 1)
    is_f = lane < DT_LANE0
    is_dt = jnp.logical_and(lane >= DT_LANE0, lane < DT_LANE0 + N_SSD_HEADS)
    r = lax.broadcasted_iota(jnp.int32, (CHUNK, CHUNK), 0)
    c = lax.broadcasted_iota(jnp.int32, (CHUNK, CHUNK), 1)
    tri = jnp.where(r >= c, 1.0, 0.0).astype(BF16)
    is_head = r < N_ATT_HEADS
    place_k = [jnp.where(jnp.logical_and(is_head, c == AUG_PIECES * r + jj), 1.0, 0.0).astype(BF16)
               for jj in range(AUG_PIECES)]
    place_q = [jnp.where(jnp.logical_and(is_head, c == AUG_Q0 + AUG_PIECES * r + jj), 1.0, 0.0).astype(BF16)
               for jj in range(AUG_PIECES)]
    a_neg = -jnp.exp(alog_ref[...])
    bias = bias_ref[...]
    carry = jnp.zeros((1, LANES), F32)
    for ci in range(seq // CHUNK):
        rows = slice(ci * CHUNK, (ci + 1) * CHUNK)
        v = small_ref[rows, :] + bias
        t = jnp.log1p(jnp.exp(-jnp.abs(v)))
        log_f = jnp.minimum(v, 0.0) - t
        dt = jnp.maximum(v, 0.0) + t
        val = jnp.where(is_f, log_f, jnp.where(is_dt, dt * a_neg, 0.0)) * LOG2E
        out = _split_dot(tri, val) + carry
        col_ref[rows, :] = out
        dt_ref[rows, :] = jnp.where(is_dt, dt, 0.0)
        row_ref[ci] = out.T
        pieces = _split3(out)
        acc_k = sum(jnp.dot(p, m, preferred_element_type=F32) for p, m in zip(pieces, place_k))
        acc_q = sum(jnp.dot(p, m, preferred_element_type=F32) for p, m in zip(pieces, place_q))
        in_k = lane < AUG_Q0
        in_q = jnp.logical_and(lane >= AUG_Q0, lane < 2 * AUG_Q0)
        augk_ref[rows, :] = jnp.where(in_k, -acc_k, jnp.where(in_q, 1.0, 0.0)).astype(BF16)
        augq_ref[rows, :] = jnp.where(in_k, 1.0, acc_q).astype(BF16)
        carry = jnp.where(is_f[:1], out[CHUNK - 1:CHUNK, :], 0.0)


def _gates(small3, bias_vec, alog_vec):
    b, seq, _ = small3.shape
    nc = seq // CHUNK
    return pl.pallas_call(
        _gate_kernel,
        grid=(b,),
        in_specs=[
            pl.BlockSpec((None, seq, LANES), lambda i: (i, 0, 0)),
            pl.BlockSpec((1, LANES), lambda i: (0, 0)),
            pl.BlockSpec((1, LANES), lambda i: (0, 0)),
        ],
        out_specs=[
            pl.BlockSpec((None, seq, LANES), lambda i: (i, 0, 0)),
            pl.BlockSpec((None, seq, LANES), lambda i: (i, 0, 0)),
            pl.BlockSpec((None, nc, CHUNK, LANES), lambda i: (i, 0, 0, 0)),
            pl.BlockSpec((None, seq, LANES), lambda i: (i, 0, 0)),
            pl.BlockSpec((None, seq, LANES), lambda i: (i, 0, 0)),
        ],
        out_shape=[
            jax.ShapeDtypeStruct((b, seq, LANES), F32),
            jax.ShapeDtypeStruct((b, seq, LANES), F32),
            jax.ShapeDtypeStruct((b, nc, CHUNK, LANES), F32),
            jax.ShapeDtypeStruct((b, seq, LANES), BF16),
            jax.ShapeDtypeStruct((b, seq, LANES), BF16),
        ],
        compiler_params=_cparams(("parallel",)),
        name="gates",
    )(small3, bias_vec, alog_vec)


TQ = 512
TK = 512
HEADS_PER_STEP = 2
KX_W = HEAD_DIM + LANES
ATT_SCALE = 1.0 / math.sqrt(HEAD_DIM)
NT_DIMS = (((1,), (1,)), ((), ()))


def _attn_kernel(q_ref, k_ref, v_ref, z_ref, aq_ref, ak_ref, o_ref,
                 kx_ref, vx_ref, qx_ref, s_ref, p_ref, acc_ref, o_scr):
    i = pl.program_id(1)
    seq = k_ref.shape[0]

    @pl.when(i == 0)
    def _():
        ak = ak_ref[...]
        ones = jnp.ones((seq, LANES), BF16)
        for h in range(N_ATT_HEADS):
            hs = slice(h * HEAD_DIM, (h + 1) * HEAD_DIM)
            kx_ref[h, :, 0:HEAD_DIM] = k_ref[:, hs]
            kx_ref[h, :, HEAD_DIM:KX_W] = ak
            vx_ref[h, :, 0:HEAD_DIM] = v_ref[:, hs]
            vx_ref[h, :, HEAD_DIM:KX_W] = ones

    lane = lax.broadcasted_iota(jnp.int32, (1, LANES), 1)
    aq = aq_ref[...].astype(F32)
    for h in range(N_ATT_HEADS):
        k_lo = AUG_PIECES * h
        q_lo = AUG_Q0 + AUG_PIECES * h
        own = jnp.logical_or(jnp.logical_and(lane >= k_lo, lane < k_lo + AUG_PIECES),
                             jnp.logical_and(lane >= q_lo, lane < q_lo + AUG_PIECES))
        qx_ref[h, :, 0:HEAD_DIM] = q_ref[:, h * HEAD_DIM:(h + 1) * HEAD_DIM]
        qx_ref[h, :, HEAD_DIM:KX_W] = (aq * jnp.where(own, 1.0, 0.0)).astype(BF16)

    rr = lax.broadcasted_iota(jnp.int32, (TQ, TK), 0)
    cc = lax.broadcasted_iota(jnp.int32, (TQ, TK), 1)
    causal = rr >= cc

    def head_group(g, n_blocks):
        heads = [g * HEADS_PER_STEP + hh for hh in range(HEADS_PER_STEP)]

        def qk(hh, j):
            s = lax.dot_general(qx_ref[heads[hh]], kx_ref[heads[hh], j * TK:(j + 1) * TK, :], NT_DIMS,
                                preferred_element_type=F32)
            if j == n_blocks - 1:
                s = jnp.where(causal, s, -jnp.inf)
            s_ref[hh, j] = s
            return jnp.max(s, axis=1, keepdims=True)

        m = [None] * HEADS_PER_STEP
        blk_max = [qk(hh, 0) for hh in range(HEADS_PER_STEP)]
        for j in range(n_blocks):
            nxt_max = [qk(hh, j + 1) for hh in range(HEADS_PER_STEP)] if j + 1 < n_blocks else None
            for hh in range(HEADS_PER_STEP):
                m_new = blk_max[hh] if j == 0 else jnp.maximum(m[hh], blk_max[hh])
                p_ref[hh, j] = jnp.exp2(s_ref[hh, j] - m_new).astype(BF16)
                pv = jnp.dot(p_ref[hh, j], vx_ref[heads[hh], j * TK:(j + 1) * TK, :], preferred_element_type=F32)
                if j == 0:
                    acc_ref[hh] = pv
                else:
                    acc_ref[hh] = jnp.exp2(m[hh] - m_new) * acc_ref[hh] + pv
                m[hh] = m_new
            blk_max = nxt_max
        for hh in range(HEADS_PER_STEP):
            acc = acc_ref[hh]
            o_scr[heads[hh]] = acc[:, 0:HEAD_DIM] / acc[:, HEAD_DIM:KX_W]

    for n_blocks in range(1, seq // TQ + 1):
        @pl.when(i == n_blocks - 1)
        def _(n_blocks=n_blocks):
            def body(g, carry):
                head_group(g, n_blocks)
                return carry
            lax.fori_loop(0, N_ATT_HEADS // HEADS_PER_STEP, body, 0)

    for h in range(N_ATT_HEADS):
        hs = slice(h * HEAD_DIM, (h + 1) * HEAD_DIM)
        z = z_ref[:, hs].astype(F32)
        o_ref[:, hs] = (o_scr[h] * (z * jax.nn.sigmoid(z))).astype(BF16)


def _attention(proj3, augq, augk):
    b, seq, _ = proj3.shape
    return pl.pallas_call(
        _attn_kernel,
        grid=(b, seq // TQ),
        in_specs=[
            pl.BlockSpec((None, TQ, D_ATT), lambda bi, i: (bi, i, 0)),
            pl.BlockSpec((None, seq, D_ATT), lambda bi, i: (bi, 0, 1), pipeline_mode=pl.Buffered(1)),
            pl.BlockSpec((None, seq, D_ATT), lambda bi, i: (bi, 0, 2), pipeline_mode=pl.Buffered(1)),
            pl.BlockSpec((None, TQ, D_ATT), lambda bi, i: (bi, i, 3)),
            pl.BlockSpec((None, TQ, LANES), lambda bi, i: (bi, i, 0)),
            pl.BlockSpec((None, seq, LANES), lambda bi, i: (bi, 0, 0)),
        ],
        out_specs=pl.BlockSpec((None, TQ, D_ATT), lambda bi, i: (bi, i, 0)),
        out_shape=jax.ShapeDtypeStruct((b, seq, D_ATT), BF16),
        scratch_shapes=[
            pltpu.VMEM((N_ATT_HEADS, seq, KX_W), BF16),
            pltpu.VMEM((N_ATT_HEADS, seq, KX_W), BF16),
            pltpu.VMEM((N_ATT_HEADS, TQ, KX_W), BF16),
            pltpu.VMEM((HEADS_PER_STEP, seq // TK, TQ, TK), F32),
            pltpu.VMEM((HEADS_PER_STEP, seq // TK, TQ, TK), BF16),
            pltpu.VMEM((HEADS_PER_STEP, TQ, KX_W), F32),
            pltpu.VMEM((N_ATT_HEADS, TQ, HEAD_DIM), F32),
        ],
        compiler_params=_cparams(("parallel", "arbitrary")),
        name="fox_attention",
    )(proj3, proj3, proj3, proj3, augq, augk)


CONV_PAD = 8
CONV_COLS = 512


def _split2_dot(val, e_bf16):
    hi = val.astype(BF16)
    lo = (val - hi.astype(F32)).astype(BF16)
    return (jnp.dot(hi, e_bf16, preferred_element_type=F32)
            + jnp.dot(lo, e_bf16, preferred_element_type=F32))


def _ssd_kernel(z_ref, xr_ref, bcr_ref, cw_ref, cb_ref, acol_ref, arow_ref, dt_ref, e_ref, dsk_ref, nw_ref,
                y_ref, buf_ref, st_ref, xs_ref, bc_ref, ex_ref, xdt_ref, xdec_ref, yacc_ref):
    c = pl.program_id(1)

    @pl.when(c == 0)
    def _():
        buf_ref[0:CONV_PAD, :] = jnp.zeros((CONV_PAD, D_XBC), F32)
        st_ref[...] = jnp.zeros_like(st_ref)

    buf_ref[CONV_PAD:CONV_PAD + CHUNK, 0:D_SSD] = xr_ref[...].astype(F32)
    buf_ref[CONV_PAD:CONV_PAD + CHUNK, D_SSD:D_XBC] = bcr_ref[...].astype(F32)
    for s in range(D_XBC // CONV_COLS):
        cs = slice(s * CONV_COLS, (s + 1) * CONV_COLS)
        u = cb_ref[:, cs]
        for kk in range(CONV_K):
            off = CONV_PAD - (CONV_K - 1) + kk
            u = u + cw_ref[kk:kk + 1, cs] * buf_ref[off:off + CHUNK, cs]
        u = u * jax.nn.sigmoid(u)
        if s * CONV_COLS < D_SSD:
            xs_ref[:, cs] = u
        else:
            bc_ref[:, s * CONV_COLS - D_SSD:(s + 1) * CONV_COLS - D_SSD] = u.astype(BF16)
    buf_ref[0:CONV_PAD, :] = buf_ref[CHUNK:CHUNK + CONV_PAD, :]

    lane = lax.broadcasted_iota(jnp.int32, (CHUNK, LANES), 1)
    is_dt = jnp.logical_and(lane >= DT_LANE0, lane < DT_LANE0 + N_SSD_HEADS)
    acs = acol_ref[...]
    a_last = acs[CHUNK - 1:CHUNK, :]
    exp_a = jnp.where(is_dt, jnp.exp2(acs), 0.0)
    dec_end = jnp.where(is_dt, jnp.exp2(a_last - acs), 0.0)
    e = e_ref[...]
    ex_ref[0] = _split2_dot(dt_ref[...], e)
    ex_ref[1] = _split2_dot(exp_a, e)
    ex_ref[2] = _split2_dot(dec_end, e)

    xdt = xs_ref[...] * ex_ref[0]
    xdt_ref[...] = xdt.astype(BF16)
    xdec_ref[...] = (xdt * ex_ref[2]).astype(BF16)

    rr = lax.broadcasted_iota(jnp.int32, (CHUNK, CHUNK), 0)
    cc = lax.broadcasted_iota(jnp.int32, (CHUNK, CHUNK), 1)
    causal = rr >= cc
    arow = arow_ref[...]
    for g in range(SSD_GROUPS):
        gs = slice(g * GROUP_W, (g + 1) * GROUP_W)
        bg = bc_ref[:, g * D_STATE:(g + 1) * D_STATE]
        cg = bc_ref[:, D_BC + g * D_STATE:D_BC + (g + 1) * D_STATE]
        cb = lax.dot_general(cg, bg, (((1,), (1,)), ((), ())), preferred_element_type=F32)
        st = st_ref[g]
        y_off = jnp.dot(cg, st.astype(BF16), preferred_element_type=F32) * ex_ref[1, :, gs]
        yacc_ref[:, gs] = y_off
        for rh in range(HEADS_PER_GROUP):
            h = g * HEADS_PER_GROUP + rh
            hsl = slice(h * SSD_HEAD_DIM, (h + 1) * SSD_HEAD_DIM)
            col = acs[:, DT_LANE0 + h:DT_LANE0 + h + 1]
            row = arow[DT_LANE0 + h:DT_LANE0 + h + 1, :]
            decay = jnp.exp2(jnp.where(causal, col - row, -jnp.inf))
            mm = (cb * decay).astype(BF16)
            yacc_ref[:, hsl] += jnp.dot(mm, xdt_ref[:, hsl], preferred_element_type=F32)
        upd = lax.dot_general(bg, xdec_ref[:, gs], (((0,), (0,)), ((), ())), preferred_element_type=F32)
        st_ref[g] = st * ex_ref[1, CHUNK - 1:CHUNK, gs] + upd

    for g in range(SSD_GROUPS):
        gs = slice(g * GROUP_W, (g + 1) * GROUP_W)
        z = z_ref[:, gs].astype(F32)
        u = (yacc_ref[:, gs] + dsk_ref[:, gs] * xs_ref[:, gs]) * (z * jax.nn.sigmoid(z))
        ms = jnp.mean(u * u, axis=-1, keepdims=True)
        y_ref[:, gs] = (u * lax.rsqrt(ms + NORM_EPS) * nw_ref[:, gs]).astype(BF16)


def _ssd(proj3, conv_w, conv_b, acol, arow, dtv, expand, dskip_e, norm_w):
    b, seq, _ = proj3.shape
    nc = seq // CHUNK
    zb = (3 * D_ATT + D_ATT) // D_SSD
    return pl.pallas_call(
        _ssd_kernel,
        grid=(b, nc),
        in_specs=[
            pl.BlockSpec((None, CHUNK, D_SSD), lambda bi, ci: (bi, ci, zb)),
            pl.BlockSpec((None, CHUNK, D_SSD), lambda bi, ci: (bi, ci, zb + 1)),
            pl.BlockSpec((None, CHUNK, D_SSD), lambda bi, ci: (bi, ci, zb + 2)),
            pl.BlockSpec((CONV_K, D_XBC), lambda bi, ci: (0, 0)),
            pl.BlockSpec((1, D_XBC), lambda bi, ci: (0, 0)),
            pl.BlockSpec((None, CHUNK, LANES), lambda bi, ci: (bi, ci, 0)),
            pl.BlockSpec((None, None, CHUNK, LANES), lambda bi, ci: (bi, ci, 0, 0)),
            pl.BlockSpec((None, CHUNK, LANES), lambda bi, ci: (bi, ci, 0)),
            pl.BlockSpec((LANES, D_SSD), lambda bi, ci: (0, 0)),
            pl.BlockSpec((1, D_SSD), lambda bi, ci: (0, 0)),
            pl.BlockSpec((1, D_SSD), lambda bi, ci: (0, 0)),
        ],
        out_specs=pl.BlockSpec((None, CHUNK, D_SSD), lambda bi, ci: (bi, ci, 0)),
        out_shape=jax.ShapeDtypeStruct((b, seq, D_SSD), BF16),
        scratch_shapes=[
            pltpu.VMEM((CONV_PAD + CHUNK, D_XBC), F32),
            pltpu.VMEM((SSD_GROUPS, D_STATE, GROUP_W), F32),
            pltpu.VMEM((CHUNK, D_SSD), F32),
            pltpu.VMEM((CHUNK, 2 * D_BC), BF16),
            pltpu.VMEM((3, CHUNK, D_SSD), F32),
            pltpu.VMEM((CHUNK, D_SSD), BF16),
            pltpu.VMEM((CHUNK, D_SSD), BF16),
            pltpu.VMEM((CHUNK, D_SSD), F32),
        ],
        compiler_params=_cparams(("parallel", "arbitrary")),
        name="ssd",
    )(proj3, proj3, proj3, conv_w, conv_b, acol, arow, dtv, expand, dskip_e, norm_w)


TM_OUT = 512


def _out_kernel(att_ref, y_ref, w_ref, x_ref, g_ref, b_ref, o_ref, *, alpha):
    h = jnp.dot(att_ref[...], w_ref[0:D_ATT, :], preferred_element_type=F32)
    h = h + jnp.dot(y_ref[...], w_ref[D_ATT:D_MIX, :], preferred_element_type=F32)
    r = alpha * x_ref[...] + h
    mu = jnp.mean(r, axis=-1, keepdims=True)
    d = r - mu
    var = jnp.mean(d * d, axis=-1, keepdims=True)
    o_ref[...] = d * lax.rsqrt(var + NORM_EPS) * g_ref[...] + b_ref[...]


def _outproj(att2, y2, w_out, x2, ln_g, ln_b, alpha):
    m = x2.shape[0]
    return pl.pallas_call(
        functools.partial(_out_kernel, alpha=alpha),
        grid=(m // TM_OUT,),
        in_specs=[
            pl.BlockSpec((TM_OUT, D_ATT), lambda i: (i, 0)),
            pl.BlockSpec((TM_OUT, D_SSD), lambda i: (i, 0)),
            pl.BlockSpec((D_MIX, D_MODEL), lambda i: (0, 0), pipeline_mode=pl.Buffered(1)),
            pl.BlockSpec((TM_OUT, D_MODEL), lambda i: (i, 0)),
            pl.BlockSpec((1, D_MODEL), lambda i: (0, 0)),
            pl.BlockSpec((1, D_MODEL), lambda i: (0, 0)),
        ],
        out_specs=pl.BlockSpec((TM_OUT, D_MODEL), lambda i: (i, 0)),
        out_shape=jax.ShapeDtypeStruct((m, D_MODEL), F32),
        compiler_params=_cparams(("parallel",)),
        name="outproj_deepnorm",
    )(att2, y2, w_out, x2, ln_g, ln_b)


def _pad_lanes(vec, lane0):
    out = jnp.zeros((1, LANES), F32)
    return out.at[0, lane0:lane0 + vec.shape[0]].set(vec.astype(F32))


def _layer(x, w_in, b_forget, conv_w, conv_b, dt_bias, a_log, d_skip, ssd_norm_w, w_out, ln_g, ln_b, alpha):
    b, seq, _ = x.shape
    x2 = x.reshape(b * seq, D_MODEL)

    q_scale = jnp.where(jnp.arange(3 * D_ATT) < D_ATT, ATT_SCALE * LOG2E, 1.0).astype(F32)
    w_qkv = (w_in[:, :_OFF_F] * q_scale[None, :]).astype(BF16)
    w_rest = w_in[:, _OFF_ZATT:_OFF_DT].astype(BF16)
    w_small = jnp.concatenate(
        [w_in[:, _OFF_F:_OFF_ZATT], w_in[:, _OFF_DT:],
         jnp.zeros((D_MODEL, LANES - N_ATT_HEADS - N_SSD_HEADS), w_in.dtype)], axis=1).astype(BF16)
    bias_vec = _pad_lanes(b_forget, F_LANE0) + _pad_lanes(dt_bias, DT_LANE0)
    alog_vec = _pad_lanes(a_log, DT_LANE0)
    heads = jnp.arange(D_SSD, dtype=jnp.int32) // SSD_HEAD_DIM
    expand = (jnp.arange(LANES, dtype=jnp.int32)[:, None] == heads[None, :] + DT_LANE0).astype(BF16)
    dskip_e = jnp.repeat(d_skip.astype(F32), SSD_HEAD_DIM)[None, :]

    proj, small = _inproj(x2, w_qkv, w_rest, w_small)
    proj3 = proj.reshape(b, seq, D_MAIN)
    gcol, dtv, grow, augq, augk = _gates(small.reshape(b, seq, LANES), bias_vec, alog_vec)

    att = _attention(proj3, augq, augk)
    y = _ssd(proj3, conv_w.astype(F32), conv_b.astype(F32)[None, :], gcol, grow, dtv, expand, dskip_e,
             ssd_norm_w.astype(F32)[None, :])
    out = _outproj(att.reshape(b * seq, D_ATT), y.reshape(b * seq, D_SSD), w_out.astype(BF16), x2,
                   ln_g.astype(F32)[None, :], ln_b.astype(F32)[None, :], alpha)
    return out.reshape(b, seq, D_MODEL)


def kernel(x, w_in, b_forget, conv_w, conv_b, dt_bias, a_log, d_skip, ssd_norm_w, w_out, ln_g, ln_b):
    depth = w_in.shape[0]
    alpha = (2.0 * depth) ** 0.25
    for i in range(depth):
        x = _layer(x, w_in[i], b_forget[i], conv_w[i], conv_b[i], dt_bias[i], a_log[i], d_skip[i],
                   ssd_norm_w[i], w_out[i], ln_g[i], ln_b[i], alpha)
    return x
```

```python
import functools
import math

import jax
import jax.numpy as jnp
from jax import lax
from jax.experimental import pallas as pl
from jax.experimental.pallas import tpu as pltpu

F32 = jnp.float32
BF16 = jnp.bfloat16

D_MODEL = 2048
D_ATT = 1024
HEAD_DIM = 128
N_ATT_HEADS = 8
D_SSD = 2048
SSD_HEAD_DIM = 64
N_SSD_HEADS = 32
SSD_GROUPS = 8
HEADS_PER_GROUP = N_SSD_HEADS // SSD_GROUPS
D_STATE = 128
CONV_K = 4
CHUNK = 128
D_BC = SSD_GROUPS * D_STATE
D_XBC = D_SSD + 2 * D_BC
D_MIX = D_ATT + D_SSD
NORM_EPS = 1e-5
LANES = 128
GROUP_W = HEADS_PER_GROUP * SSD_HEAD_DIM

_OFF_F = 3 * D_ATT
_OFF_ZATT = _OFF_F + N_ATT_HEADS
_OFF_DT = _OFF_ZATT + D_ATT + D_SSD + D_XBC
D_MAIN = 3 * D_ATT + D_ATT + D_SSD + D_XBC
F_LANE0 = 0
DT_LANE0 = N_ATT_HEADS

AUG_PIECES = 3
AUG_Q0 = AUG_PIECES * N_ATT_HEADS
LOG2E = math.log2(math.e)

VMEM_LIMIT = 56 * 1024 * 1024


def _cparams(sem, **kw):
    return pltpu.CompilerParams(dimension_semantics=sem, vmem_limit_bytes=VMEM_LIMIT, **kw)


TR_PREP = 256
ATT_SCALE = 1.0 / math.sqrt(HEAD_DIM)
_LAST_TILE0 = (_OFF_DT // LANES) * LANES
_DT_IN_TILE = _OFF_DT - _LAST_TILE0


def _wprep_kernel(w_ref, wm_ref, ws_ref):
    wm_ref[:, 0:D_ATT] = (w_ref[:, 0:D_ATT] * (ATT_SCALE * LOG2E)).astype(BF16)
    wm_ref[:, D_ATT:_OFF_F] = w_ref[:, D_ATT:_OFF_F].astype(BF16)
    wm_ref[:, _OFF_F:D_MAIN] = w_ref[:, _OFF_ZATT:_OFF_DT].astype(BF16)
    ws_ref[...] = jnp.zeros(ws_ref.shape, BF16)
    n_tail = w_ref.shape[1] - _LAST_TILE0
    ws_ref[:, 0:n_tail] = w_ref[:, _LAST_TILE0:].astype(BF16)
    ws_ref[:, 0:N_ATT_HEADS] = w_ref[:, _OFF_F:_OFF_ZATT].astype(BF16)


def _wprep(w_in):
    k, n = w_in.shape
    return pl.pallas_call(
        _wprep_kernel,
        grid=(k // TR_PREP,),
        in_specs=[pl.BlockSpec((TR_PREP, n), lambda i: (i, 0))],
        out_specs=[pl.BlockSpec((TR_PREP, D_MAIN), lambda i: (i, 0)),
                   pl.BlockSpec((TR_PREP, LANES), lambda i: (i, 0))],
        out_shape=[jax.ShapeDtypeStruct((k, D_MAIN), BF16), jax.ShapeDtypeStruct((k, LANES), BF16)],
        compiler_params=_cparams(("parallel",)),
        name="weight_prep",
    )(w_in)


TM_IN = 1024
TN_IN = 1024


def _inproj_kernel(x_ref, w_ref, ws_ref, proj_ref, small_ref, xb_ref):
    @pl.when(pl.program_id(1) == 0)
    def _():
        xb = x_ref[...].astype(BF16)
        xb_ref[...] = xb
        small_ref[...] = jnp.dot(xb, ws_ref[...], preferred_element_type=F32)

    proj_ref[...] = jnp.dot(xb_ref[...], w_ref[...], preferred_element_type=F32).astype(BF16)


def _inproj(x2, w_main, w_small):
    m = x2.shape[0]
    return pl.pallas_call(
        _inproj_kernel,
        grid=(m // TM_IN, D_MAIN // TN_IN),
        in_specs=[
            pl.BlockSpec((TM_IN, D_MODEL), lambda i, j: (i, 0)),
            pl.BlockSpec((D_MODEL, TN_IN), lambda i, j: (0, j)),
            pl.BlockSpec((D_MODEL, LANES), lambda i, j: (0, 0)),
        ],
        out_specs=[
            pl.BlockSpec((TM_IN, TN_IN), lambda i, j: (i, j)),
            pl.BlockSpec((TM_IN, LANES), lambda i, j: (i, 0)),
        ],
        out_shape=[
            jax.ShapeDtypeStruct((m, D_MAIN), BF16),
            jax.ShapeDtypeStruct((m, LANES), F32),
        ],
        scratch_shapes=[pltpu.VMEM((TM_IN, D_MODEL), BF16)],
        compiler_params=_cparams(("parallel", "arbitrary")),
        name="inproj",
    )(x2, w_main, w_small)


def _split_dot(t_bf16, val):
    hi = val.astype(BF16)
    r1 = val - hi.astype(F32)
    mid = r1.astype(BF16)
    lo = (r1 - mid.astype(F32)).astype(BF16)
    return (jnp.dot(t_bf16, hi, preferred_element_type=F32)
            + jnp.dot(t_bf16, mid, preferred_element_type=F32)
            + jnp.dot(t_bf16, lo, preferred_element_type=F32))


def _split3(val):
    hi = val.astype(BF16)
    r1 = val - hi.astype(F32)
    mid = r1.astype(BF16)
    lo = (r1 - mid.astype(F32)).astype(BF16)
    return hi, mid, lo


def _gate_kernel(small_ref, bias_ref, alog_ref, col_ref, dt_ref, row_ref, augq_ref, augk_ref):
    seq = small_ref.shape[0]
    lane = lax.broadcasted_iota(jnp.int32, (CHUNK, LANES), 1)
    is_f = lane < DT_LANE0
    is_dt = jnp.logical_and(lane >= DT_LANE0, lane < DT_LANE0 + N_SSD_HEADS)
    r = lax.broadcasted_iota(jnp.int32, (CHUNK, CHUNK), 0)
    c = lax.broadcasted_iota(jnp.int32, (CHUNK, CHUNK), 1)
    tri = jnp.where(r >= c, 1.0, 0.0).astype(BF16)
    is_head = r < N_ATT_HEADS
    place_k = [jnp.where(jnp.logical_and(is_head, c == AUG_PIECES * r + jj), 1.0, 0.0).astype(BF16)
               for jj in range(AUG_PIECES)]
    place_q = [jnp.where(jnp.logical_and(is_head, c == AUG_Q0 + AUG_PIECES * r + jj), 1.0, 0.0).astype(BF16)
               for jj in range(AUG_PIECES)]
    a_neg = -jnp.exp(alog_ref[...])
    bias = bias_ref[...]
    carry = jnp.zeros((1, LANES), F32)
    for ci in range(seq // CHUNK):
        rows = slice(ci * CHUNK, (ci + 1) * CHUNK)
        v = small_ref[rows, :] + bias
        t = jnp.log1p(jnp.exp(-jnp.abs(v)))
        log_f = jnp.minimum(v, 0.0) - t
        dt = jnp.maximum(v, 0.0) + t
        val = jnp.where(is_f, log_f, jnp.where(is_dt, dt * a_neg, 0.0)) * LOG2E
        out = _split_dot(tri, val) + carry
        col_ref[rows, :] = out
        dt_ref[rows, :] = jnp.where(is_dt, dt, 0.0)
        row_ref[ci] = out.T
        pieces = _split3(out)
        acc_k = sum(jnp.dot(p, m, preferred_element_type=F32) for p, m in zip(pieces, place_k))
        acc_q = sum(jnp.dot(p, m, preferred_element_type=F32) for p, m in zip(pieces, place_q))
        in_k = lane < AUG_Q0
        in_q = jnp.logical_and(lane >= AUG_Q0, lane < 2 * AUG_Q0)
        augk_ref[rows, :] = jnp.where(in_k, -acc_k, jnp.where(in_q, 1.0, 0.0)).astype(BF16)
        augq_ref[rows, :] = jnp.where(in_k, 1.0, acc_q).astype(BF16)
        carry = jnp.where(is_f[:1], out[CHUNK - 1:CHUNK, :], 0.0)


def _gates(small3, bias_vec, alog_vec):
    b, seq, _ = small3.shape
    nc = seq // CHUNK
    return pl.pallas_call(
        _gate_kernel,
        grid=(b,),
        in_specs=[
            pl.BlockSpec((None, seq, LANES), lambda i: (i, 0, 0)),
            pl.BlockSpec((1, LANES), lambda i: (0, 0)),
            pl.BlockSpec((1, LANES), lambda i: (0, 0)),
        ],
        out_specs=[
            pl.BlockSpec((None, seq, LANES), lambda i: (i, 0, 0)),
            pl.BlockSpec((None, seq, LANES), lambda i: (i, 0, 0)),
            pl.BlockSpec((None, nc, CHUNK, LANES), lambda i: (i, 0, 0, 0)),
            pl.BlockSpec((None, seq, LANES), lambda i: (i, 0, 0)),
            pl.BlockSpec((None, seq, LANES), lambda i: (i, 0, 0)),
        ],
        out_shape=[
            jax.ShapeDtypeStruct((b, seq, LANES), F32),
            jax.ShapeDtypeStruct((b, seq, LANES), F32),
            jax.ShapeDtypeStruct((b, nc, CHUNK, LANES), F32),
            jax.ShapeDtypeStruct((b, seq, LANES), BF16),
            jax.ShapeDtypeStruct((b, seq, LANES), BF16),
        ],
        compiler_params=_cparams(("parallel",)),
        name="gates",
    )(small3, bias_vec, alog_vec)


TQ = 512
TK = 512
HEADS_PER_STEP = 2
KX_W = HEAD_DIM + LANES
NT_DIMS = (((1,), (1,)), ((), ()))


def _attn_kernel(q_ref, k_ref, v_ref, z_ref, aq_ref, ak_ref, o_ref,
                 kx_ref, vx_ref, qx_ref, s_ref, p_ref, acc_ref, o_scr):
    i = pl.program_id(1)
    seq = k_ref.shape[0]

    @pl.when(i == 0)
    def _():
        ak = ak_ref[...]
        ones = jnp.ones((seq, LANES), BF16)
        for h in range(N_ATT_HEADS):
            hs = slice(h * HEAD_DIM, (h + 1) * HEAD_DIM)
            kx_ref[h, :, 0:HEAD_DIM] = k_ref[:, hs]
            kx_ref[h, :, HEAD_DIM:KX_W] = ak
            vx_ref[h, :, 0:HEAD_DIM] = v_ref[:, hs]
            vx_ref[h, :, HEAD_DIM:KX_W] = ones

    lane = lax.broadcasted_iota(jnp.int32, (1, LANES), 1)
    aq = aq_ref[...].astype(F32)
    for h in range(N_ATT_HEADS):
        k_lo = AUG_PIECES * h
        q_lo = AUG_Q0 + AUG_PIECES * h
        own = jnp.logical_or(jnp.logical_and(lane >= k_lo, lane < k_lo + AUG_PIECES),
                             jnp.logical_and(lane >= q_lo, lane < q_lo + AUG_PIECES))
        qx_ref[h, :, 0:HEAD_DIM] = q_ref[:, h * HEAD_DIM:(h + 1) * HEAD_DIM]
        qx_ref[h, :, HEAD_DIM:KX_W] = (aq * jnp.where(own, 1.0, 0.0)).astype(BF16)

    rr = lax.broadcasted_iota(jnp.int32, (TQ, TK), 0)
    cc = lax.broadcasted_iota(jnp.int32, (TQ, TK), 1)
    causal = rr >= cc

    def head_group(g, n_blocks):
        heads = [g * HEADS_PER_STEP + hh for hh in range(HEADS_PER_STEP)]

        def qk(hh, j):
            s = lax.dot_general(qx_ref[heads[hh]], kx_ref[heads[hh], j * TK:(j + 1) * TK, :], NT_DIMS,
                                preferred_element_type=F32)
            if j == n_blocks - 1:
                s = jnp.where(causal, s, -jnp.inf)
            s_ref[hh, j] = s
            return jnp.max(s, axis=1, keepdims=True)

        m = [None] * HEADS_PER_STEP
        blk_max = [qk(hh, 0) for hh in range(HEADS_PER_STEP)]
        for j in range(n_blocks):
            nxt_max = [qk(hh, j + 1) for hh in range(HEADS_PER_STEP)] if j + 1 < n_blocks else None
            for hh in range(HEADS_PER_STEP):
                m_new = blk_max[hh] if j == 0 else jnp.maximum(m[hh], blk_max[hh])
                p_ref[hh, j] = jnp.exp2(s_ref[hh, j] - m_new).astype(BF16)
                pv = jnp.dot(p_ref[hh, j], vx_ref[heads[hh], j * TK:(j + 1) * TK, :], preferred_element_type=F32)
                if j == 0:
                    acc_ref[hh] = pv
                else:
                    acc_ref[hh] = jnp.exp2(m[hh] - m_new) * acc_ref[hh] + pv
                m[hh] = m_new
            blk_max = nxt_max
        for hh in range(HEADS_PER_STEP):
            acc = acc_ref[hh]
            o_scr[heads[hh]] = acc[:, 0:HEAD_DIM] / acc[:, HEAD_DIM:KX_W]

    for n_blocks in range(1, seq // TQ + 1):
        @pl.when(i == n_blocks - 1)
        def _(n_blocks=n_blocks):
            def body(g, carry):
                head_group(g, n_blocks)
                return carry
            lax.fori_loop(0, N_ATT_HEADS // HEADS_PER_STEP, body, 0)

    for h in range(N_ATT_HEADS):
        hs = slice(h * HEAD_DIM, (h + 1) * HEAD_DIM)
        z = z_ref[:, hs].astype(F32)
        o_ref[:, hs] = (o_scr[h] * (z * jax.nn.sigmoid(z))).astype(BF16)


def _attention(proj3, augq, augk):
    b, seq, _ = proj3.shape
    return pl.pallas_call(
        _attn_kernel,
        grid=(b, seq // TQ),
        in_specs=[
            pl.BlockSpec((None, TQ, D_ATT), lambda bi, i: (bi, i, 0)),
            pl.BlockSpec((None, seq, D_ATT), lambda bi, i: (bi, 0, 1), pipeline_mode=pl.Buffered(1)),
            pl.BlockSpec((None, seq, D_ATT), lambda bi, i: (bi, 0, 2), pipeline_mode=pl.Buffered(1)),
            pl.BlockSpec((None, TQ, D_ATT), lambda bi, i: (bi, i, 3)),
            pl.BlockSpec((None, TQ, LANES), lambda bi, i: (bi, i, 0)),
            pl.BlockSpec((None, seq, LANES), lambda bi, i: (bi, 0, 0)),
        ],
        out_specs=pl.BlockSpec((None, TQ, D_ATT), lambda bi, i: (bi, i, 0)),
        out_shape=jax.ShapeDtypeStruct((b, seq, D_ATT), BF16),
        scratch_shapes=[
            pltpu.VMEM((N_ATT_HEADS, seq, KX_W), BF16),
            pltpu.VMEM((N_ATT_HEADS, seq, KX_W), BF16),
            pltpu.VMEM((N_ATT_HEADS, TQ, KX_W), BF16),
            pltpu.VMEM((HEADS_PER_STEP, seq // TK, TQ, TK), F32),
            pltpu.VMEM((HEADS_PER_STEP, seq // TK, TQ, TK), BF16),
            pltpu.VMEM((HEADS_PER_STEP, TQ, KX_W), F32),
            pltpu.VMEM((N_ATT_HEADS, TQ, HEAD_DIM), F32),
        ],
        compiler_params=_cparams(("parallel", "arbitrary")),
        name="fox_attention",
    )(proj3, proj3, proj3, proj3, augq, augk)


SSD_BATCH = 2
HIST = 16
N_SHIFT = CONV_K - 1
CONV_COLS = 512


def _split2_dot(val, e_bf16):
    hi = val.astype(BF16)
    lo = (val - hi.astype(F32)).astype(BF16)
    return (jnp.dot(hi, e_bf16, preferred_element_type=F32)
            + jnp.dot(lo, e_bf16, preferred_element_type=F32))


def _ssd_kernel(z_ref, xr_ref, bcr_ref, cw_ref, cb_ref, acol_ref, arow_ref, dt_ref, e_ref, dsk_ref, nw_ref,
                y_ref, uext_ref, hist_ref, sh_ref, st_ref, xs_ref, bc_ref, ex_ref, xdt_ref, xdec_ref, yacc_ref):
    c = pl.program_id(1)
    seqs = range(SSD_BATCH)

    @pl.when(c == 0)
    def _():
        hist_ref[...] = jnp.zeros_like(hist_ref)
        st_ref[...] = jnp.zeros_like(st_ref)

    out_row = lax.broadcasted_iota(jnp.int32, (N_SHIFT * CHUNK, HIST + CHUNK), 0)
    src_row = lax.broadcasted_iota(jnp.int32, (N_SHIFT * CHUNK, HIST + CHUNK), 1)
    shift = out_row // CHUNK + 1
    shift_mat = jnp.where(src_row == HIST + out_row % CHUNK - shift, 1.0, 0.0).astype(BF16)
    for e in seqs:
        uext_ref[e, 0:HIST, :] = hist_ref[e]
        uext_ref[e, HIST:HIST + CHUNK, 0:D_SSD] = xr_ref[e]
        uext_ref[e, HIST:HIST + CHUNK, D_SSD:D_XBC] = bcr_ref[e]
        hist_ref[e, :, 0:D_SSD] = xr_ref[e, CHUNK - HIST:CHUNK, :]
        hist_ref[e, :, D_SSD:D_XBC] = bcr_ref[e, CHUNK - HIST:CHUNK, :]
    for e in seqs:
        sh_ref[e] = jnp.dot(shift_mat, uext_ref[e], preferred_element_type=F32)
    for s in range(D_XBC // CONV_COLS):
        cs = slice(s * CONV_COLS, (s + 1) * CONV_COLS)
        for e in seqs:
            u = cb_ref[:, cs] + cw_ref[CONV_K - 1:CONV_K, cs] * uext_ref[e, HIST:HIST + CHUNK, cs].astype(F32)
            for kk in range(N_SHIFT):
                blk = N_SHIFT - 1 - kk
                u = u + cw_ref[kk:kk + 1, cs] * sh_ref[e, blk * CHUNK:(blk + 1) * CHUNK, cs]
            u = u * jax.nn.sigmoid(u)
            if s * CONV_COLS < D_SSD:
                xs_ref[e, :, cs] = u
            else:
                bc_ref[e, :, s * CONV_COLS - D_SSD:(s + 1) * CONV_COLS - D_SSD] = u.astype(BF16)

    lane = lax.broadcasted_iota(jnp.int32, (CHUNK, LANES), 1)
    is_dt = jnp.logical_and(lane >= DT_LANE0, lane < DT_LANE0 + N_SSD_HEADS)
    expand = e_ref[...]
    for e in seqs:
        acs = acol_ref[e]
        a_last = acs[CHUNK - 1:CHUNK, :]
        exp_a = jnp.where(is_dt, jnp.exp2(acs), 0.0)
        dec_end = jnp.where(is_dt, jnp.exp2(a_last - acs), 0.0)
        ex_ref[e, 0] = _split2_dot(dt_ref[e], expand)
        ex_ref[e, 1] = _split2_dot(exp_a, expand)
        ex_ref[e, 2] = _split2_dot(dec_end, expand)
    for e in seqs:
        xdt = xs_ref[e] * ex_ref[e, 0]
        xdt_ref[e] = xdt.astype(BF16)
        xdec_ref[e] = (xdt * ex_ref[e, 2]).astype(BF16)

    rr = lax.broadcasted_iota(jnp.int32, (CHUNK, CHUNK), 0)
    cc = lax.broadcasted_iota(jnp.int32, (CHUNK, CHUNK), 1)
    causal = rr >= cc
    for g in range(SSD_GROUPS):
        gs = slice(g * GROUP_W, (g + 1) * GROUP_W)
        for e in seqs:
            acs = acol_ref[e]
            arow = arow_ref[e]
            bg = bc_ref[e, :, g * D_STATE:(g + 1) * D_STATE]
            cg = bc_ref[e, :, D_BC + g * D_STATE:D_BC + (g + 1) * D_STATE]
            cb = lax.dot_general(cg, bg, NT_DIMS, preferred_element_type=F32)
            st = st_ref[e, g]
            y_off = jnp.dot(cg, st.astype(BF16), preferred_element_type=F32) * ex_ref[e, 1, :, gs]
            yacc_ref[e, :, gs] = y_off
            for rh in range(HEADS_PER_GROUP):
                h = g * HEADS_PER_GROUP + rh
                hsl = slice(h * SSD_HEAD_DIM, (h + 1) * SSD_HEAD_DIM)
                col = acs[:, DT_LANE0 + h:DT_LANE0 + h + 1]
                row = arow[DT_LANE0 + h:DT_LANE0 + h + 1, :]
                decay = jnp.exp2(jnp.where(causal, col - row, -jnp.inf))
                mm = (cb * decay).astype(BF16)
                yacc_ref[e, :, hsl] += jnp.dot(mm, xdt_ref[e, :, hsl], preferred_element_type=F32)
            upd = lax.dot_general(bg, xdec_ref[e, :, gs], (((0,), (0,)), ((), ())), preferred_element_type=F32)
            st_ref[e, g] = st * ex_ref[e, 1, CHUNK - 1:CHUNK, gs] + upd

    for g in range(SSD_GROUPS):
        gs = slice(g * GROUP_W, (g + 1) * GROUP_W)
        for e in seqs:
            z = z_ref[e, :, gs].astype(F32)
            u = (yacc_ref[e, :, gs] + dsk_ref[:, gs] * xs_ref[e, :, gs]) * (z * jax.nn.sigmoid(z))
            ms = jnp.mean(u * u, axis=-1, keepdims=True)
            y_ref[e, :, gs] = (u * lax.rsqrt(ms + NORM_EPS) * nw_ref[:, gs]).astype(BF16)


def _ssd(proj3, conv_w, conv_b, acol, arow, dtv, expand, dskip_e, norm_w):
    b, seq, _ = proj3.shape
    nc = seq // CHUNK
    nb = SSD_BATCH
    zb = (3 * D_ATT + D_ATT) // D_SSD
    return pl.pallas_call(
        _ssd_kernel,
        grid=(b // nb, nc),
        in_specs=[
            pl.BlockSpec((nb, CHUNK, D_SSD), lambda bi, ci: (bi, ci, zb)),
            pl.BlockSpec((nb, CHUNK, D_SSD), lambda bi, ci: (bi, ci, zb + 1)),
            pl.BlockSpec((nb, CHUNK, D_SSD), lambda bi, ci: (bi, ci, zb + 2)),
            pl.BlockSpec((CONV_K, D_XBC), lambda bi, ci: (0, 0)),
            pl.BlockSpec((1, D_XBC), lambda bi, ci: (0, 0)),
            pl.BlockSpec((nb, CHUNK, LANES), lambda bi, ci: (bi, ci, 0)),
            pl.BlockSpec((nb, None, CHUNK, LANES), lambda bi, ci: (bi, ci, 0, 0)),
            pl.BlockSpec((nb, CHUNK, LANES), lambda bi, ci: (bi, ci, 0)),
            pl.BlockSpec((LANES, D_SSD), lambda bi, ci: (0, 0)),
            pl.BlockSpec((1, D_SSD), lambda bi, ci: (0, 0)),
            pl.BlockSpec((1, D_SSD), lambda bi, ci: (0, 0)),
        ],
        out_specs=pl.BlockSpec((nb, CHUNK, D_SSD), lambda bi, ci: (bi, ci, 0)),
        out_shape=jax.ShapeDtypeStruct((b, seq, D_SSD), BF16),
        scratch_shapes=[
            pltpu.VMEM((nb, HIST + CHUNK, D_XBC), BF16),
            pltpu.VMEM((nb, HIST, D_XBC), BF16),
            pltpu.VMEM((nb, N_SHIFT * CHUNK, D_XBC), F32),
            pltpu.VMEM((nb, SSD_GROUPS, D_STATE, GROUP_W), F32),
            pltpu.VMEM((nb, CHUNK, D_SSD), F32),
            pltpu.VMEM((nb, CHUNK, 2 * D_BC), BF16),
            pltpu.VMEM((nb, 3, CHUNK, D_SSD), F32),
            pltpu.VMEM((nb, CHUNK, D_SSD), BF16),
            pltpu.VMEM((nb, CHUNK, D_SSD), BF16),
            pltpu.VMEM((nb, CHUNK, D_SSD), F32),
        ],
        compiler_params=_cparams(("parallel", "arbitrary")),
        name="ssd",
    )(proj3, proj3, proj3, conv_w, conv_b, acol, arow, dtv, expand, dskip_e, norm_w)


TM_OUT = 512


def _out_kernel(att_ref, y_ref, w_ref, x_ref, g_ref, b_ref, o_ref, *, alpha):
    h = jnp.dot(att_ref[...], w_ref[0:D_ATT, :], preferred_element_type=F32)
    h = h + jnp.dot(y_ref[...], w_ref[D_ATT:D_MIX, :], preferred_element_type=F32)
    r = alpha * x_ref[...] + h
    mu = jnp.mean(r, axis=-1, keepdims=True)
    d = r - mu
    var = jnp.mean(d * d, axis=-1, keepdims=True)
    o_ref[...] = d * lax.rsqrt(var + NORM_EPS) * g_ref[...] + b_ref[...]


def _outproj(att2, y2, w_out, x2, ln_g, ln_b, alpha):
    m = x2.shape[0]
    return pl.pallas_call(
        functools.partial(_out_kernel, alpha=alpha),
        grid=(m // TM_OUT,),
        in_specs=[
            pl.BlockSpec((TM_OUT, D_ATT), lambda i: (i, 0)),
            pl.BlockSpec((TM_OUT, D_SSD), lambda i: (i, 0)),
            pl.BlockSpec((D_MIX, D_MODEL), lambda i: (0, 0), pipeline_mode=pl.Buffered(1)),
            pl.BlockSpec((TM_OUT, D_MODEL), lambda i: (i, 0)),
            pl.BlockSpec((1, D_MODEL), lambda i: (0, 0)),
            pl.BlockSpec((1, D_MODEL), lambda i: (0, 0)),
        ],
        out_specs=pl.BlockSpec((TM_OUT, D_MODEL), lambda i: (i, 0)),
        out_shape=jax.ShapeDtypeStruct((m, D_MODEL), F32),
        compiler_params=_cparams(("parallel",)),
        name="outproj_deepnorm",
    )(att2, y2, w_out, x2, ln_g, ln_b)


def _pad_lanes(vec, lane0):
    out = jnp.zeros((1, LANES), F32)
    return out.at[0, lane0:lane0 + vec.shape[0]].set(vec.astype(F32))


def _layer(x, w_in, b_forget, conv_w, conv_b, dt_bias, a_log, d_skip, ssd_norm_w, w_out, ln_g, ln_b, alpha):
    b, seq, _ = x.shape
    x2 = x.reshape(b * seq, D_MODEL)

    assert _DT_IN_TILE == DT_LANE0 and F_LANE0 == 0
    w_main, w_small = _wprep(w_in)
    bias_vec = _pad_lanes(b_forget, F_LANE0) + _pad_lanes(dt_bias, DT_LANE0)
    alog_vec = _pad_lanes(a_log, DT_LANE0)
    heads = jnp.arange(D_SSD, dtype=jnp.int32) // SSD_HEAD_DIM
    expand = (jnp.arange(LANES, dtype=jnp.int32)[:, None] == heads[None, :] + DT_LANE0).astype(BF16)
    dskip_e = jnp.repeat(d_skip.astype(F32), SSD_HEAD_DIM)[None, :]

    proj, small = _inproj(x2, w_main, w_small)
    proj3 = proj.reshape(b, seq, D_MAIN)
    gcol, dtv, grow, augq, augk = _gates(small.reshape(b, seq, LANES), bias_vec, alog_vec)

    att = _attention(proj3, augq, augk)
    y = _ssd(proj3, conv_w.astype(F32), conv_b.astype(F32)[None, :], gcol, grow, dtv, expand, dskip_e,
             ssd_norm_w.astype(F32)[None, :])
    out = _outproj(att.reshape(b * seq, D_ATT), y.reshape(b * seq, D_SSD), w_out.astype(BF16), x2,
                   ln_g.astype(F32)[None, :], ln_b.astype(F32)[None, :], alpha)
    return out.reshape(b, seq, D_MODEL)


def kernel(x, w_in, b_forget, conv_w, conv_b, dt_bias, a_log, d_skip, ssd_norm_w, w_out, ln_g, ln_b):
    depth = w_in.shape[0]
    alpha = (2.0 * depth) ** 0.25
    for i in range(depth):
        x = _layer(x, w_in[i], b_forget[i], conv_w[i], conv_b[i], dt_bias[i], a_log[i], d_skip[i],
                   ssd_norm_w[i], w_out[i], ln_g[i], ln_b[i], alpha)
    return x
```

```python
import functools
import math

import jax
import jax.numpy as jnp
from jax import lax
from jax.experimental import pallas as pl
from jax.experimental.pallas import tpu as pltpu

F32 = jnp.float32
BF16 = jnp.bfloat16

D_MODEL = 2048
D_ATT = 1024
HEAD_DIM = 128
N_ATT_HEADS = 8
D_SSD = 2048
SSD_HEAD_DIM = 64
N_SSD_HEADS = 32
SSD_GROUPS = 8
HEADS_PER_GROUP = N_SSD_HEADS // SSD_GROUPS
D_STATE = 128
CONV_K = 4
CHUNK = 128
D_BC = SSD_GROUPS * D_STATE
D_XBC = D_SSD + 2 * D_BC
D_MIX = D_ATT + D_SSD
NORM_EPS = 1e-5
LANES = 128
GROUP_W = HEADS_PER_GROUP * SSD_HEAD_DIM

_OFF_F = 3 * D_ATT
_OFF_ZATT = _OFF_F + N_ATT_HEADS
_OFF_DT = _OFF_ZATT + D_ATT + D_SSD + D_XBC
D_MAIN = 3 * D_ATT + D_ATT + D_SSD + D_XBC
F_LANE0 = 0
DT_LANE0 = N_ATT_HEADS

AUG_PIECES = 3
AUG_Q0 = AUG_PIECES * N_ATT_HEADS
LOG2E = math.log2(math.e)

VMEM_LIMIT = 56 * 1024 * 1024


def _cparams(sem, **kw):
    return pltpu.CompilerParams(dimension_semantics=sem, vmem_limit_bytes=VMEM_LIMIT, **kw)


TK_PREP = 256
ATT_SCALE = 1.0 / math.sqrt(HEAD_DIM)
NT_DIMS = (((1,), (1,)), ((), ()))


def _wprep_kernel(wt_ref, wm_ref, ws_ref):
    wm_ref[0:D_ATT, :] = (wt_ref[0:D_ATT, :] * (ATT_SCALE * LOG2E)).astype(BF16)
    wm_ref[D_ATT:_OFF_F, :] = wt_ref[D_ATT:_OFF_F, :].astype(BF16)
    wm_ref[_OFF_F:D_MAIN, :] = wt_ref[_OFF_ZATT:_OFF_DT, :].astype(BF16)
    small = jnp.concatenate(
        [wt_ref[_OFF_F:_OFF_ZATT, :], wt_ref[_OFF_DT:_OFF_DT + N_SSD_HEADS, :],
         jnp.zeros((LANES - N_ATT_HEADS - N_SSD_HEADS, TK_PREP), F32)], axis=0)
    ws_ref[...] = small.astype(BF16)


def _wprep(w_t):
    n, k = w_t.shape
    return pl.pallas_call(
        _wprep_kernel,
        grid=(k // TK_PREP,),
        in_specs=[pl.BlockSpec((n, TK_PREP), lambda i: (0, i))],
        out_specs=[pl.BlockSpec((D_MAIN, TK_PREP), lambda i: (0, i)),
                   pl.BlockSpec((LANES, TK_PREP), lambda i: (0, i))],
        out_shape=[jax.ShapeDtypeStruct((D_MAIN, k), BF16), jax.ShapeDtypeStruct((LANES, k), BF16)],
        compiler_params=_cparams(("parallel",)),
        name="weight_prep",
    )(w_t)


TM_IN = 1024
TN_IN = 1024


def _inproj_kernel(x_ref, w_ref, ws_ref, proj_ref, small_ref, xb_ref):
    @pl.when(pl.program_id(1) == 0)
    def _():
        xb = x_ref[...].astype(BF16)
        xb_ref[...] = xb
        small_ref[...] = lax.dot_general(xb, ws_ref[...], NT_DIMS, preferred_element_type=F32)

    proj_ref[...] = lax.dot_general(xb_ref[...], w_ref[...], NT_DIMS, preferred_element_type=F32).astype(BF16)


def _inproj(x2, w_main_t, w_small_t):
    m = x2.shape[0]
    return pl.pallas_call(
        _inproj_kernel,
        grid=(m // TM_IN, D_MAIN // TN_IN),
        in_specs=[
            pl.BlockSpec((TM_IN, D_MODEL), lambda i, j: (i, 0)),
            pl.BlockSpec((TN_IN, D_MODEL), lambda i, j: (j, 0)),
            pl.BlockSpec((LANES, D_MODEL), lambda i, j: (0, 0)),
        ],
        out_specs=[
            pl.BlockSpec((TM_IN, TN_IN), lambda i, j: (i, j)),
            pl.BlockSpec((TM_IN, LANES), lambda i, j: (i, 0)),
        ],
        out_shape=[
            jax.ShapeDtypeStruct((m, D_MAIN), BF16),
            jax.ShapeDtypeStruct((m, LANES), F32),
        ],
        scratch_shapes=[pltpu.VMEM((TM_IN, D_MODEL), BF16)],
        compiler_params=_cparams(("parallel", "arbitrary")),
        name="inproj",
    )(x2, w_main_t, w_small_t)


def _split_dot(t_bf16, val):
    hi = val.astype(BF16)
    r1 = val - hi.astype(F32)
    mid = r1.astype(BF16)
    lo = (r1 - mid.astype(F32)).astype(BF16)
    return (jnp.dot(t_bf16, hi, preferred_element_type=F32)
            + jnp.dot(t_bf16, mid, preferred_element_type=F32)
            + jnp.dot(t_bf16, lo, preferred_element_type=F32))


def _split3(val):
    hi = val.astype(BF16)
    r1 = val - hi.astype(F32)
    mid = r1.astype(BF16)
    lo = (r1 - mid.astype(F32)).astype(BF16)
    return hi, mid, lo


def _gate_kernel(small_ref, bias_ref, alog_ref, col_ref, dt_ref, row_ref, augq_ref, augk_ref):
    seq = small_ref.shape[0]
    lane = lax.broadcasted_iota(jnp.int32, (CHUNK, LANES), 1)
    is_f = lane < DT_LANE0
    is_dt = jnp.logical_and(lane >= DT_LANE0, lane < DT_LANE0 + N_SSD_HEADS)
    r = lax.broadcasted_iota(jnp.int32, (CHUNK, CHUNK), 0)
    c = lax.broadcasted_iota(jnp.int32, (CHUNK, CHUNK), 1)
    tri = jnp.where(r >= c, 1.0, 0.0).astype(BF16)
    is_head = r < N_ATT_HEADS
    place_k = [jnp.where(jnp.logical_and(is_head, c == AUG_PIECES * r + jj), 1.0, 0.0).astype(BF16)
               for jj in range(AUG_PIECES)]
    place_q = [jnp.where(jnp.logical_and(is_head, c == AUG_Q0 + AUG_PIECES * r + jj), 1.0, 0.0).astype(BF16)
               for jj in range(AUG_PIECES)]
    a_neg = -jnp.exp(alog_ref[...])
    bias = bias_ref[...]
    carry = jnp.zeros((1, LANES), F32)
    for ci in range(seq // CHUNK):
        rows = slice(ci * CHUNK, (ci + 1) * CHUNK)
        v = small_ref[rows, :] + bias
        t = jnp.log1p(jnp.exp(-jnp.abs(v)))
        log_f = jnp.minimum(v, 0.0) - t
        dt = jnp.maximum(v, 0.0) + t
        val = jnp.where(is_f, log_f, jnp.where(is_dt, dt * a_neg, 0.0)) * LOG2E
        out = _split_dot(tri, val) + carry
        col_ref[rows, :] = out
        dt_ref[rows, :] = jnp.where(is_dt, dt, 0.0)
        row_ref[ci] = out.T
        pieces = _split3(out)
        acc_k = sum(jnp.dot(p, m, preferred_element_type=F32) for p, m in zip(pieces, place_k))
        acc_q = sum(jnp.dot(p, m, preferred_element_type=F32) for p, m in zip(pieces, place_q))
        in_k = lane < AUG_Q0
        in_q = jnp.logical_and(lane >= AUG_Q0, lane < 2 * AUG_Q0)
        augk_ref[rows, :] = jnp.where(in_k, -acc_k, jnp.where(in_q, 1.0, 0.0)).astype(BF16)
        augq_ref[rows, :] = jnp.where(in_k, 1.0, acc_q).astype(BF16)
        carry = jnp.where(is_f[:1], out[CHUNK - 1:CHUNK, :], 0.0)


def _gates(small3, bias_vec, alog_vec):
    b, seq, _ = small3.shape
    nc = seq // CHUNK
    return pl.pallas_call(
        _gate_kernel,
        grid=(b,),
        in_specs=[
            pl.BlockSpec((None, seq, LANES), lambda i: (i, 0, 0)),
            pl.BlockSpec((1, LANES), lambda i: (0, 0)),
            pl.BlockSpec((1, LANES), lambda i: (0, 0)),
        ],
        out_specs=[
            pl.BlockSpec((None, seq, LANES), lambda i: (i, 0, 0)),
            pl.BlockSpec((None, seq, LANES), lambda i: (i, 0, 0)),
            pl.BlockSpec((None, nc, CHUNK, LANES), lambda i: (i, 0, 0, 0)),
            pl.BlockSpec((None, seq, LANES), lambda i: (i, 0, 0)),
            pl.BlockSpec((None, seq, LANES), lambda i: (i, 0, 0)),
        ],
        out_shape=[
            jax.ShapeDtypeStruct((b, seq, LANES), F32),
            jax.ShapeDtypeStruct((b, seq, LANES), F32),
            jax.ShapeDtypeStruct((b, nc, CHUNK, LANES), F32),
            jax.ShapeDtypeStruct((b, seq, LANES), BF16),
            jax.ShapeDtypeStruct((b, seq, LANES), BF16),
        ],
        compiler_params=_cparams(("parallel",)),
        name="gates",
    )(small3, bias_vec, alog_vec)


TQ = 512
TK = 512
HEADS_PER_STEP = 2
KX_W = HEAD_DIM + LANES


def _attn_kernel(q_ref, k_ref, v_ref, z_ref, aq_ref, ak_ref, o_ref,
                 kx_ref, vx_ref, qx_ref, s_ref, p_ref, acc_ref, o_scr):
    i = pl.program_id(1)
    seq = k_ref.shape[0]

    @pl.when(i == 0)
    def _():
        ak = ak_ref[...]
        ones = jnp.ones((seq, LANES), BF16)
        for h in range(N_ATT_HEADS):
            hs = slice(h * HEAD_DIM, (h + 1) * HEAD_DIM)
            kx_ref[h, :, 0:HEAD_DIM] = k_ref[:, hs]
            kx_ref[h, :, HEAD_DIM:KX_W] = ak
            vx_ref[h, :, 0:HEAD_DIM] = v_ref[:, hs]
            vx_ref[h, :, HEAD_DIM:KX_W] = ones

    lane = lax.broadcasted_iota(jnp.int32, (1, LANES), 1)
    aq = aq_ref[...].astype(F32)
    for h in range(N_ATT_HEADS):
        k_lo = AUG_PIECES * h
        q_lo = AUG_Q0 + AUG_PIECES * h
        own = jnp.logical_or(jnp.logical_and(lane >= k_lo, lane < k_lo + AUG_PIECES),
                             jnp.logical_and(lane >= q_lo, lane < q_lo + AUG_PIECES))
        qx_ref[h, :, 0:HEAD_DIM] = q_ref[:, h * HEAD_DIM:(h + 1) * HEAD_DIM]
        qx_ref[h, :, HEAD_DIM:KX_W] = (aq * jnp.where(own, 1.0, 0.0)).astype(BF16)

    rr = lax.broadcasted_iota(jnp.int32, (TQ, TK), 0)
    cc = lax.broadcasted_iota(jnp.int32, (TQ, TK), 1)
    causal = rr >= cc

    def head_group(g, n_blocks):
        heads = [g * HEADS_PER_STEP + hh for hh in range(HEADS_PER_STEP)]

        def qk(hh, j):
            s = lax.dot_general(qx_ref[heads[hh]], kx_ref[heads[hh], j * TK:(j + 1) * TK, :], NT_DIMS,
                                preferred_element_type=F32)
            if j == n_blocks - 1:
                s = jnp.where(causal, s, -jnp.inf)
            s_ref[hh, j] = s
            return jnp.max(s, axis=1, keepdims=True)

        m = [None] * HEADS_PER_STEP
        blk_max = [qk(hh, 0) for hh in range(HEADS_PER_STEP)]
        for j in range(n_blocks):
            nxt_max = [qk(hh, j + 1) for hh in range(HEADS_PER_STEP)] if j + 1 < n_blocks else None
            for hh in range(HEADS_PER_STEP):
                m_new = blk_max[hh] if j == 0 else jnp.maximum(m[hh], blk_max[hh])
                p_ref[hh, j] = jnp.exp2(s_ref[hh, j] - m_new).astype(BF16)
                pv = jnp.dot(p_ref[hh, j], vx_ref[heads[hh], j * TK:(j + 1) * TK, :], preferred_element_type=F32)
                if j == 0:
                    acc_ref[hh] = pv
                else:
                    acc_ref[hh] = jnp.exp2(m[hh] - m_new) * acc_ref[hh] + pv
                m[hh] = m_new
            blk_max = nxt_max
        for hh in range(HEADS_PER_STEP):
            acc = acc_ref[hh]
            o_scr[heads[hh]] = acc[:, 0:HEAD_DIM] / acc[:, HEAD_DIM:KX_W]

    for n_blocks in range(1, seq // TQ + 1):
        @pl.when(i == n_blocks - 1)
        def _(n_blocks=n_blocks):
            def body(g, carry):
                head_group(g, n_blocks)
                return carry
            lax.fori_loop(0, N_ATT_HEADS // HEADS_PER_STEP, body, 0)

    for h in range(N_ATT_HEADS):
        hs = slice(h * HEAD_DIM, (h + 1) * HEAD_DIM)
        z = z_ref[:, hs].astype(F32)
        o_ref[:, hs] = (o_scr[h] * (z * jax.nn.sigmoid(z))).astype(BF16)


def _attention(proj3, augq, augk):
    b, seq, _ = proj3.shape
    return pl.pallas_call(
        _attn_kernel,
        grid=(b, seq // TQ),
        in_specs=[
            pl.BlockSpec((None, TQ, D_ATT), lambda bi, i: (bi, i, 0)),
            pl.BlockSpec((None, seq, D_ATT), lambda bi, i: (bi, 0, 1), pipeline_mode=pl.Buffered(1)),
            pl.BlockSpec((None, seq, D_ATT), lambda bi, i: (bi, 0, 2), pipeline_mode=pl.Buffered(1)),
            pl.BlockSpec((None, TQ, D_ATT), lambda bi, i: (bi, i, 3)),
            pl.BlockSpec((None, TQ, LANES), lambda bi, i: (bi, i, 0)),
            pl.BlockSpec((None, seq, LANES), lambda bi, i: (bi, 0, 0)),
        ],
        out_specs=pl.BlockSpec((None, TQ, D_ATT), lambda bi, i: (bi, i, 0)),
        out_shape=jax.ShapeDtypeStruct((b, seq, D_ATT), BF16),
        scratch_shapes=[
            pltpu.VMEM((N_ATT_HEADS, seq, KX_W), BF16),
            pltpu.VMEM((N_ATT_HEADS, seq, KX_W), BF16),
            pltpu.VMEM((N_ATT_HEADS, TQ, KX_W), BF16),
            pltpu.VMEM((HEADS_PER_STEP, seq // TK, TQ, TK), F32),
            pltpu.VMEM((HEADS_PER_STEP, seq // TK, TQ, TK), BF16),
            pltpu.VMEM((HEADS_PER_STEP, TQ, KX_W), F32),
            pltpu.VMEM((N_ATT_HEADS, TQ, HEAD_DIM), F32),
        ],
        compiler_params=_cparams(("parallel", "arbitrary")),
        name="fox_attention",
    )(proj3, proj3, proj3, proj3, augq, augk)


SSD_BATCH = 2
HIST = 16
N_SHIFT = CONV_K - 1
CONV_COLS = 512


def _split2_dot(val, e_bf16):
    hi = val.astype(BF16)
    lo = (val - hi.astype(F32)).astype(BF16)
    return (jnp.dot(hi, e_bf16, preferred_element_type=F32)
            + jnp.dot(lo, e_bf16, preferred_element_type=F32))


def _ssd_kernel(z_ref, xr_ref, bcr_ref, cw_ref, cb_ref, acol_ref, arow_ref, dt_ref, e_ref, dsk_ref, nw_ref,
                y_ref, uext_ref, hist_ref, sh_ref, st_ref, xs_ref, bc_ref, ex_ref, xdt_ref, xdec_ref, yacc_ref):
    c = pl.program_id(1)
    seqs = range(SSD_BATCH)

    @pl.when(c == 0)
    def _():
        hist_ref[...] = jnp.zeros_like(hist_ref)
        st_ref[...] = jnp.zeros_like(st_ref)

    out_row = lax.broadcasted_iota(jnp.int32, (N_SHIFT * CHUNK, HIST + CHUNK), 0)
    src_row = lax.broadcasted_iota(jnp.int32, (N_SHIFT * CHUNK, HIST + CHUNK), 1)
    shift = out_row // CHUNK + 1
    shift_mat = jnp.where(src_row == HIST + out_row % CHUNK - shift, 1.0, 0.0).astype(BF16)
    for e in seqs:
        uext_ref[e, 0:HIST, :] = hist_ref[e]
        uext_ref[e, HIST:HIST + CHUNK, 0:D_SSD] = xr_ref[e]
        uext_ref[e, HIST:HIST + CHUNK, D_SSD:D_XBC] = bcr_ref[e]
        hist_ref[e, :, 0:D_SSD] = xr_ref[e, CHUNK - HIST:CHUNK, :]
        hist_ref[e, :, D_SSD:D_XBC] = bcr_ref[e, CHUNK - HIST:CHUNK, :]
    for e in seqs:
        sh_ref[e] = jnp.dot(shift_mat, uext_ref[e], preferred_element_type=F32)
    for s in range(D_XBC // CONV_COLS):
        cs = slice(s * CONV_COLS, (s + 1) * CONV_COLS)
        for e in seqs:
            u = cb_ref[:, cs] + cw_ref[CONV_K - 1:CONV_K, cs] * uext_ref[e, HIST:HIST + CHUNK, cs].astype(F32)
            for kk in range(N_SHIFT):
                blk = N_SHIFT - 1 - kk
                u = u + cw_ref[kk:kk + 1, cs] * sh_ref[e, blk * CHUNK:(blk + 1) * CHUNK, cs]
            u = u * jax.nn.sigmoid(u)
            if s * CONV_COLS < D_SSD:
                xs_ref[e, :, cs] = u
            else:
                bc_ref[e, :, s * CONV_COLS - D_SSD:(s + 1) * CONV_COLS - D_SSD] = u.astype(BF16)

    lane = lax.broadcasted_iota(jnp.int32, (CHUNK, LANES), 1)
    is_dt = jnp.logical_and(lane >= DT_LANE0, lane < DT_LANE0 + N_SSD_HEADS)
    expand = e_ref[...]
    for e in seqs:
        acs = acol_ref[e]
        a_last = acs[CHUNK - 1:CHUNK, :]
        exp_a = jnp.where(is_dt, jnp.exp2(acs), 0.0)
        dec_end = jnp.where(is_dt, jnp.exp2(a_last - acs), 0.0)
        ex_ref[e, 0] = _split2_dot(dt_ref[e], expand)
        ex_ref[e, 1] = _split2_dot(exp_a, expand)
        ex_ref[e, 2] = _split2_dot(dec_end, expand)
    for e in seqs:
        xdt = xs_ref[e] * ex_ref[e, 0]
        xdt_ref[e] = xdt.astype(BF16)
        xdec_ref[e] = (xdt * ex_ref[e, 2]).astype(BF16)

    rr = lax.broadcasted_iota(jnp.int32, (CHUNK, CHUNK), 0)
    cc = lax.broadcasted_iota(jnp.int32, (CHUNK, CHUNK), 1)
    causal = rr >= cc
    for g in range(SSD_GROUPS):
        gs = slice(g * GROUP_W, (g + 1) * GROUP_W)
        for e in seqs:
            acs = acol_ref[e]
            arow = arow_ref[e]
            bg = bc_ref[e, :, g * D_STATE:(g + 1) * D_STATE]
            cg = bc_ref[e, :, D_BC + g * D_STATE:D_BC + (g + 1) * D_STATE]
            cb = lax.dot_general(cg, bg, NT_DIMS, preferred_element_type=F32)
            st = st_ref[e, g]
            y_off = jnp.dot(cg, st.astype(BF16), preferred_element_type=F32) * ex_ref[e, 1, :, gs]
            yacc_ref[e, :, gs] = y_off
            for rh in range(HEADS_PER_GROUP):
                h = g * HEADS_PER_GROUP + rh
                hsl = slice(h * SSD_HEAD_DIM, (h + 1) * SSD_HEAD_DIM)
                col = acs[:, DT_LANE0 + h:DT_LANE0 + h + 1]
                row = arow[DT_LANE0 + h:DT_LANE0 + h + 1, :]
                decay = jnp.exp2(jnp.where(causal, col - row, -jnp.inf))
                mm = (cb * decay).astype(BF16)
                yacc_ref[e, :, hsl] += jnp.dot(mm, xdt_ref[e, :, hsl], preferred_element_type=F32)
            upd = lax.dot_general(bg, xdec_ref[e, :, gs], (((0,), (0,)), ((), ())), preferred_element_type=F32)
            st_ref[e, g] = st * ex_ref[e, 1, CHUNK - 1:CHUNK, gs] + upd

    for g in range(SSD_GROUPS):
        gs = slice(g * GROUP_W, (g + 1) * GROUP_W)
        for e in seqs:
            z = z_ref[e, :, gs].astype(F32)
            u = (yacc_ref[e, :, gs] + dsk_ref[:, gs] * xs_ref[e, :, gs]) * (z * jax.nn.sigmoid(z))
            ms = jnp.mean(u * u, axis=-1, keepdims=True)
            y_ref[e, :, gs] = (u * lax.rsqrt(ms + NORM_EPS) * nw_ref[:, gs]).astype(BF16)


def _ssd(proj3, conv_w, conv_b, acol, arow, dtv, expand, dskip_e, norm_w):
    b, seq, _ = proj3.shape
    nc = seq // CHUNK
    nb = SSD_BATCH
    zb = (3 * D_ATT + D_ATT) // D_SSD
    return pl.pallas_call(
        _ssd_kernel,
        grid=(b // nb, nc),
        in_specs=[
            pl.BlockSpec((nb, CHUNK, D_SSD), lambda bi, ci: (bi, ci, zb)),
            pl.BlockSpec((nb, CHUNK, D_SSD), lambda bi, ci: (bi, ci, zb + 1)),
            pl.BlockSpec((nb, CHUNK, D_SSD), lambda bi, ci: (bi, ci, zb + 2)),
            pl.BlockSpec((CONV_K, D_XBC), lambda bi, ci: (0, 0)),
            pl.BlockSpec((1, D_XBC), lambda bi, ci: (0, 0)),
            pl.BlockSpec((nb, CHUNK, LANES), lambda bi, ci: (bi, ci, 0)),
            pl.BlockSpec((nb, None, CHUNK, LANES), lambda bi, ci: (bi, ci, 0, 0)),
            pl.BlockSpec((nb, CHUNK, LANES), lambda bi, ci: (bi, ci, 0)),
            pl.BlockSpec((LANES, D_SSD), lambda bi, ci: (0, 0)),
            pl.BlockSpec((1, D_SSD), lambda bi, ci: (0, 0)),
            pl.BlockSpec((1, D_SSD), lambda bi, ci: (0, 0)),
        ],
        out_specs=pl.BlockSpec((nb, CHUNK, D_SSD), lambda bi, ci: (bi, ci, 0)),
        out_shape=jax.ShapeDtypeStruct((b, seq, D_SSD), BF16),
        scratch_shapes=[
            pltpu.VMEM((nb, HIST + CHUNK, D_XBC), BF16),
            pltpu.VMEM((nb, HIST, D_XBC), BF16),
            pltpu.VMEM((nb, N_SHIFT * CHUNK, D_XBC), F32),
            pltpu.VMEM((nb, SSD_GROUPS, D_STATE, GROUP_W), F32),
            pltpu.VMEM((nb, CHUNK, D_SSD), F32),
            pltpu.VMEM((nb, CHUNK, 2 * D_BC), BF16),
            pltpu.VMEM((nb, 3, CHUNK, D_SSD), F32),
            pltpu.VMEM((nb, CHUNK, D_SSD), BF16),
            pltpu.VMEM((nb, CHUNK, D_SSD), BF16),
            pltpu.VMEM((nb, CHUNK, D_SSD), F32),
        ],
        compiler_params=_cparams(("parallel", "arbitrary")),
        name="ssd",
    )(proj3, proj3, proj3, conv_w, conv_b, acol, arow, dtv, expand, dskip_e, norm_w)


TM_OUT = 512


def _out_kernel(att_ref, y_ref, w_ref, x_ref, g_ref, b_ref, o_ref, *, alpha):
    h = jnp.dot(att_ref[...], w_ref[0:D_ATT, :], preferred_element_type=F32)
    h = h + jnp.dot(y_ref[...], w_ref[D_ATT:D_MIX, :], preferred_element_type=F32)
    r = alpha * x_ref[...] + h
    mu = jnp.mean(r, axis=-1, keepdims=True)
    d = r - mu
    var = jnp.mean(d * d, axis=-1, keepdims=True)
    o_ref[...] = d * lax.rsqrt(var + NORM_EPS) * g_ref[...] + b_ref[...]


def _outproj(att2, y2, w_out, x2, ln_g, ln_b, alpha):
    m = x2.shape[0]
    return pl.pallas_call(
        functools.partial(_out_kernel, alpha=alpha),
        grid=(m // TM_OUT,),
        in_specs=[
            pl.BlockSpec((TM_OUT, D_ATT), lambda i: (i, 0)),
            pl.BlockSpec((TM_OUT, D_SSD), lambda i: (i, 0)),
            pl.BlockSpec((D_MIX, D_MODEL), lambda i: (0, 0), pipeline_mode=pl.Buffered(1)),
            pl.BlockSpec((TM_OUT, D_MODEL), lambda i: (i, 0)),
            pl.BlockSpec((1, D_MODEL), lambda i: (0, 0)),
            pl.BlockSpec((1, D_MODEL), lambda i: (0, 0)),
        ],
        out_specs=pl.BlockSpec((TM_OUT, D_MODEL), lambda i: (i, 0)),
        out_shape=jax.ShapeDtypeStruct((m, D_MODEL), F32),
        compiler_params=_cparams(("parallel",)),
        name="outproj_deepnorm",
    )(att2, y2, w_out, x2, ln_g, ln_b)


def _pad_lanes(vec, lane0):
    out = jnp.zeros((1, LANES), F32)
    return out.at[0, lane0:lane0 + vec.shape[0]].set(vec.astype(F32))


def _layer(x, w_in, b_forget, conv_w, conv_b, dt_bias, a_log, d_skip, ssd_norm_w, w_out, ln_g, ln_b, alpha):
    b, seq, _ = x.shape
    x2 = x.reshape(b * seq, D_MODEL)

    w_main_t, w_small_t = _wprep(w_in.T)
    bias_vec = _pad_lanes(b_forget, F_LANE0) + _pad_lanes(dt_bias, DT_LANE0)
    alog_vec = _pad_lanes(a_log, DT_LANE0)
    heads = jnp.arange(D_SSD, dtype=jnp.int32) // SSD_HEAD_DIM
    expand = (jnp.arange(LANES, dtype=jnp.int32)[:, None] == heads[None, :] + DT_LANE0).astype(BF16)
    dskip_e = jnp.repeat(d_skip.astype(F32), SSD_HEAD_DIM)[None, :]

    proj, small = _inproj(x2, w_main_t, w_small_t)
    proj3 = proj.reshape(b, seq, D_MAIN)
    gcol, dtv, grow, augq, augk = _gates(small.reshape(b, seq, LANES), bias_vec, alog_vec)

    att = _attention(proj3, augq, augk)
    y = _ssd(proj3, conv_w.astype(F32), conv_b.astype(F32)[None, :], gcol, grow, dtv, expand, dskip_e,
             ssd_norm_w.astype(F32)[None, :])
    out = _outproj(att.reshape(b * seq, D_ATT), y.reshape(b * seq, D_SSD), w_out.astype(BF16), x2,
                   ln_g.astype(F32)[None, :], ln_b.astype(F32)[None, :], alpha)
    return out.reshape(b, seq, D_MODEL)


def kernel(x, w_in, b_forget, conv_w, conv_b, dt_bias, a_log, d_skip, ssd_norm_w, w_out, ln_g, ln_b):
    depth = w_in.shape[0]
    alpha = (2.0 * depth) ** 0.25
    for i in range(depth):
        x = _layer(x, w_in[i], b_forget[i], conv_w[i], conv_b[i], dt_bias[i], a_log[i], d_skip[i],
                   ssd_norm_w[i], w_out[i], ln_g[i], ln_b[i], alpha)
    return x
```

```python
import functools
import math

import jax
import jax.numpy as jnp
from jax import lax
from jax.experimental import pallas as pl
from jax.experimental.pallas import tpu as pltpu

F32 = jnp.float32
BF16 = jnp.bfloat16

D_MODEL = 2048
D_ATT = 1024
HEAD_DIM = 128
N_ATT_HEADS = 8
D_SSD = 2048
SSD_HEAD_DIM = 64
N_SSD_HEADS = 32
SSD_GROUPS = 8
HEADS_PER_GROUP = N_SSD_HEADS // SSD_GROUPS
D_STATE = 128
CONV_K = 4
CHUNK = 128
D_BC = SSD_GROUPS * D_STATE
D_XBC = D_SSD + 2 * D_BC
D_MIX = D_ATT + D_SSD
NORM_EPS = 1e-5
LANES = 128
GROUP_W = HEADS_PER_GROUP * SSD_HEAD_DIM

_OFF_F = 3 * D_ATT
_OFF_ZATT = _OFF_F + N_ATT_HEADS
_OFF_DT = _OFF_ZATT + D_ATT + D_SSD + D_XBC
D_MAIN = 3 * D_ATT + D_ATT + D_SSD + D_XBC
F_LANE0 = 0
DT_LANE0 = N_ATT_HEADS

AUG_PIECES = 3
AUG_Q0 = AUG_PIECES * N_ATT_HEADS
LOG2E = math.log2(math.e)

VMEM_LIMIT = 56 * 1024 * 1024


def _cparams(sem, **kw):
    return pltpu.CompilerParams(dimension_semantics=sem, vmem_limit_bytes=VMEM_LIMIT, **kw)


TK_PREP = 256
ATT_SCALE = 1.0 / math.sqrt(HEAD_DIM)
NT_DIMS = (((1,), (1,)), ((), ()))


def _wprep_kernel(wt_ref, wm_ref, ws_ref):
    wm_ref[0:D_ATT, :] = (wt_ref[0:D_ATT, :] * (ATT_SCALE * LOG2E)).astype(BF16)
    wm_ref[D_ATT:_OFF_F, :] = wt_ref[D_ATT:_OFF_F, :].astype(BF16)
    wm_ref[_OFF_F:D_MAIN, :] = wt_ref[_OFF_ZATT:_OFF_DT, :].astype(BF16)
    small = jnp.concatenate(
        [wt_ref[_OFF_F:_OFF_ZATT, :], wt_ref[_OFF_DT:_OFF_DT + N_SSD_HEADS, :],
         jnp.zeros((LANES - N_ATT_HEADS - N_SSD_HEADS, TK_PREP), F32)], axis=0)
    ws_ref[...] = small.astype(BF16)


def _wprep(w_t):
    n, k = w_t.shape
    return pl.pallas_call(
        _wprep_kernel,
        grid=(k // TK_PREP,),
        in_specs=[pl.BlockSpec((n, TK_PREP), lambda i: (0, i))],
        out_specs=[pl.BlockSpec((D_MAIN, TK_PREP), lambda i: (0, i)),
                   pl.BlockSpec((LANES, TK_PREP), lambda i: (0, i))],
        out_shape=[jax.ShapeDtypeStruct((D_MAIN, k), BF16), jax.ShapeDtypeStruct((LANES, k), BF16)],
        compiler_params=_cparams(("parallel",)),
        name="weight_prep",
    )(w_t)


TM_IN = 1024
TN_IN = 2048


def _inproj_kernel(x_ref, w_ref, ws_ref, proj_ref, small_ref, xb_ref):
    @pl.when(pl.program_id(1) == 0)
    def _():
        xb = x_ref[...].astype(BF16)
        xb_ref[...] = xb
        small_ref[...] = lax.dot_general(xb, ws_ref[...], NT_DIMS, preferred_element_type=F32)

    proj_ref[...] = lax.dot_general(xb_ref[...], w_ref[...], NT_DIMS, preferred_element_type=F32).astype(BF16)


def _inproj(x2, w_main_t, w_small_t):
    m = x2.shape[0]
    return pl.pallas_call(
        _inproj_kernel,
        grid=(m // TM_IN, D_MAIN // TN_IN),
        in_specs=[
            pl.BlockSpec((TM_IN, D_MODEL), lambda i, j: (i, 0)),
            pl.BlockSpec((TN_IN, D_MODEL), lambda i, j: (j, 0)),
            pl.BlockSpec((LANES, D_MODEL), lambda i, j: (0, 0)),
        ],
        out_specs=[
            pl.BlockSpec((TM_IN, TN_IN), lambda i, j: (i, j)),
            pl.BlockSpec((TM_IN, LANES), lambda i, j: (i, 0)),
        ],
        out_shape=[
            jax.ShapeDtypeStruct((m, D_MAIN), BF16),
            jax.ShapeDtypeStruct((m, LANES), F32),
        ],
        scratch_shapes=[pltpu.VMEM((TM_IN, D_MODEL), BF16)],
        compiler_params=_cparams(("parallel", "arbitrary")),
        name="inproj",
    )(x2, w_main_t, w_small_t)


def _split_dot(t_bf16, val):
    hi = val.astype(BF16)
    r1 = val - hi.astype(F32)
    mid = r1.astype(BF16)
    lo = (r1 - mid.astype(F32)).astype(BF16)
    return (jnp.dot(t_bf16, hi, preferred_element_type=F32)
            + jnp.dot(t_bf16, mid, preferred_element_type=F32)
            + jnp.dot(t_bf16, lo, preferred_element_type=F32))


def _split3(val):
    hi = val.astype(BF16)
    r1 = val - hi.astype(F32)
    mid = r1.astype(BF16)
    lo = (r1 - mid.astype(F32)).astype(BF16)
    return hi, mid, lo


def _gate_kernel(small_ref, bias_ref, alog_ref, col_ref, dt_ref, row_ref, augq_ref, augk_ref):
    seq = small_ref.shape[0]
    lane = lax.broadcasted_iota(jnp.int32, (CHUNK, LANES), 1)
    is_f = lane < DT_LANE0
    is_dt = jnp.logical_and(lane >= DT_LANE0, lane < DT_LANE0 + N_SSD_HEADS)
    r = lax.broadcasted_iota(jnp.int32, (CHUNK, CHUNK), 0)
    c = lax.broadcasted_iota(jnp.int32, (CHUNK, CHUNK), 1)
    tri = jnp.where(r >= c, 1.0, 0.0).astype(BF16)
    is_head = r < N_ATT_HEADS
    place_k = [jnp.where(jnp.logical_and(is_head, c == AUG_PIECES * r + jj), 1.0, 0.0).astype(BF16)
               for jj in range(AUG_PIECES)]
    place_q = [jnp.where(jnp.logical_and(is_head, c == AUG_Q0 + AUG_PIECES * r + jj), 1.0, 0.0).astype(BF16)
               for jj in range(AUG_PIECES)]
    a_neg = -jnp.exp(alog_ref[...])
    bias = bias_ref[...]
    carry = jnp.zeros((1, LANES), F32)
    for ci in range(seq // CHUNK):
        rows = slice(ci * CHUNK, (ci + 1) * CHUNK)
        v = small_ref[rows, :] + bias
        t = jnp.log1p(jnp.exp(-jnp.abs(v)))
        log_f = jnp.minimum(v, 0.0) - t
        dt = jnp.maximum(v, 0.0) + t
        val = jnp.where(is_f, log_f, jnp.where(is_dt, dt * a_neg, 0.0)) * LOG2E
        out = _split_dot(tri, val) + carry
        col_ref[rows, :] = out
        dt_ref[rows, :] = jnp.where(is_dt, dt, 0.0)
        row_ref[ci] = out.T
        pieces = _split3(out)
        acc_k = sum(jnp.dot(p, m, preferred_element_type=F32) for p, m in zip(pieces, place_k))
        acc_q = sum(jnp.dot(p, m, preferred_element_type=F32) for p, m in zip(pieces, place_q))
        in_k = lane < AUG_Q0
        in_q = jnp.logical_and(lane >= AUG_Q0, lane < 2 * AUG_Q0)
        augk_ref[rows, :] = jnp.where(in_k, -acc_k, jnp.where(in_q, 1.0, 0.0)).astype(BF16)
        augq_ref[rows, :] = jnp.where(in_k, 1.0, acc_q).astype(BF16)
        carry = jnp.where(is_f[:1], out[CHUNK - 1:CHUNK, :], 0.0)


def _gates(small3, bias_vec, alog_vec):
    b, seq, _ = small3.shape
    nc = seq // CHUNK
    return pl.pallas_call(
        _gate_kernel,
        grid=(b,),
        in_specs=[
            pl.BlockSpec((None, seq, LANES), lambda i: (i, 0, 0)),
            pl.BlockSpec((1, LANES), lambda i: (0, 0)),
            pl.BlockSpec((1, LANES), lambda i: (0, 0)),
        ],
        out_specs=[
            pl.BlockSpec((None, seq, LANES), lambda i: (i, 0, 0)),
            pl.BlockSpec((None, seq, LANES), lambda i: (i, 0, 0)),
            pl.BlockSpec((None, nc, CHUNK, LANES), lambda i: (i, 0, 0, 0)),
            pl.BlockSpec((None, seq, LANES), lambda i: (i, 0, 0)),
            pl.BlockSpec((None, seq, LANES), lambda i: (i, 0, 0)),
        ],
        out_shape=[
            jax.ShapeDtypeStruct((b, seq, LANES), F32),
            jax.ShapeDtypeStruct((b, seq, LANES), F32),
            jax.ShapeDtypeStruct((b, nc, CHUNK, LANES), F32),
            jax.ShapeDtypeStruct((b, seq, LANES), BF16),
            jax.ShapeDtypeStruct((b, seq, LANES), BF16),
        ],
        compiler_params=_cparams(("parallel",)),
        name="gates",
    )(small3, bias_vec, alog_vec)


TQ = 512
TK = 512
HEADS_PER_STEP = 2
KX_W = HEAD_DIM + LANES


def _attn_kernel(q_ref, k_ref, v_ref, z_ref, aq_ref, ak_ref, o_ref,
                 kx_ref, vx_ref, qx_ref, s_ref, p_ref, acc_ref, o_scr):
    i = pl.program_id(1)
    seq = k_ref.shape[0]

    @pl.when(i == 0)
    def _():
        ak = ak_ref[...]
        ones = jnp.ones((seq, LANES), BF16)
        for h in range(N_ATT_HEADS):
            hs = slice(h * HEAD_DIM, (h + 1) * HEAD_DIM)
            kx_ref[h, :, 0:HEAD_DIM] = k_ref[:, hs]
            kx_ref[h, :, HEAD_DIM:KX_W] = ak
            vx_ref[h, :, 0:HEAD_DIM] = v_ref[:, hs]
            vx_ref[h, :, HEAD_DIM:KX_W] = ones

    lane = lax.broadcasted_iota(jnp.int32, (1, LANES), 1)
    aq = aq_ref[...].astype(F32)
    for h in range(N_ATT_HEADS):
        k_lo = AUG_PIECES * h
        q_lo = AUG_Q0 + AUG_PIECES * h
        own = jnp.logical_or(jnp.logical_and(lane >= k_lo, lane < k_lo + AUG_PIECES),
                             jnp.logical_and(lane >= q_lo, lane < q_lo + AUG_PIECES))
        qx_ref[h, :, 0:HEAD_DIM] = q_ref[:, h * HEAD_DIM:(h + 1) * HEAD_DIM]
        qx_ref[h, :, HEAD_DIM:KX_W] = (aq * jnp.where(own, 1.0, 0.0)).astype(BF16)

    top_rows, bot_rows = slice(0, TQ // 2), slice(TQ // 2, TQ)
    top_keys, bot_keys = slice(0, TK // 2), slice(0, TK)
    rt = lax.broadcasted_iota(jnp.int32, (TQ // 2, TK // 2), 0)
    ct = lax.broadcasted_iota(jnp.int32, (TQ // 2, TK // 2), 1)
    rb = lax.broadcasted_iota(jnp.int32, (TQ // 2, TK), 0)
    cb = lax.broadcasted_iota(jnp.int32, (TQ // 2, TK), 1)
    diag_parts = ((top_rows, top_keys, rt >= ct), (bot_rows, bot_keys, rb + TQ // 2 >= cb))

    def head_group(g, n_blocks):
        heads = [g * HEADS_PER_STEP + hh for hh in range(HEADS_PER_STEP)]
        last = n_blocks - 1

        def qk(hh, j):
            h = heads[hh]
            if j < last:
                s = lax.dot_general(qx_ref[h], kx_ref[h, j * TK:(j + 1) * TK, :], NT_DIMS,
                                    preferred_element_type=F32)
                s_ref[hh, j] = s
                return jnp.max(s, axis=1, keepdims=True)
            maxes = []
            for rows, keys, visible in diag_parts:
                s = lax.dot_general(qx_ref[h, rows, :], kx_ref[h, j * TK + keys.start:j * TK + keys.stop, :],
                                    NT_DIMS, preferred_element_type=F32)
                s = jnp.where(visible, s, -jnp.inf)
                s_ref[hh, j, rows, keys] = s
                maxes.append(jnp.max(s, axis=1, keepdims=True))
            return maxes

        m = [None] * HEADS_PER_STEP
        blk_max = [qk(hh, 0) for hh in range(HEADS_PER_STEP)]
        for j in range(n_blocks):
            nxt_max = [qk(hh, j + 1) for hh in range(HEADS_PER_STEP)] if j + 1 < n_blocks else None
            for hh in range(HEADS_PER_STEP):
                h = heads[hh]
                if j < last:
                    m_new = blk_max[hh] if j == 0 else jnp.maximum(m[hh], blk_max[hh])
                    p_ref[hh, j] = jnp.exp2(s_ref[hh, j] - m_new).astype(BF16)
                    pv = jnp.dot(p_ref[hh, j], vx_ref[h, j * TK:(j + 1) * TK, :], preferred_element_type=F32)
                    if j == 0:
                        acc_ref[hh] = pv
                    else:
                        acc_ref[hh] = jnp.exp2(m[hh] - m_new) * acc_ref[hh] + pv
                    m[hh] = m_new
                    continue
                for (rows, keys, _), part_max in zip(diag_parts, blk_max[hh]):
                    m_old = None if j == 0 else m[hh][rows]
                    m_new = part_max if j == 0 else jnp.maximum(m_old, part_max)
                    p_ref[hh, j, rows, keys] = jnp.exp2(s_ref[hh, j, rows, keys] - m_new).astype(BF16)
                    pv = jnp.dot(p_ref[hh, j, rows, keys], vx_ref[h, j * TK + keys.start:j * TK + keys.stop, :],
                                 preferred_element_type=F32)
                    if j == 0:
                        acc_ref[hh, rows] = pv
                    else:
                        acc_ref[hh, rows] = jnp.exp2(m_old - m_new) * acc_ref[hh, rows] + pv
            blk_max = nxt_max
        for hh in range(HEADS_PER_STEP):
            acc = acc_ref[hh]
            o_scr[heads[hh]] = acc[:, 0:HEAD_DIM] / acc[:, HEAD_DIM:KX_W]

    for n_blocks in range(1, seq // TQ + 1):
        @pl.when(i == n_blocks - 1)
        def _(n_blocks=n_blocks):
            def body(g, carry):
                head_group(g, n_blocks)
                return carry
            lax.fori_loop(0, N_ATT_HEADS // HEADS_PER_STEP, body, 0)

    for h in range(N_ATT_HEADS):
        hs = slice(h * HEAD_DIM, (h + 1) * HEAD_DIM)
        z = z_ref[:, hs].astype(F32)
        o_ref[:, hs] = (o_scr[h] * (z * jax.nn.sigmoid(z))).astype(BF16)


def _attention(proj3, augq, augk):
    b, seq, _ = proj3.shape
    return pl.pallas_call(
        _attn_kernel,
        grid=(b, seq // TQ),
        in_specs=[
            pl.BlockSpec((None, TQ, D_ATT), lambda bi, i: (bi, i, 0)),
            pl.BlockSpec((None, seq, D_ATT), lambda bi, i: (bi, 0, 1), pipeline_mode=pl.Buffered(1)),
            pl.BlockSpec((None, seq, D_ATT), lambda bi, i: (bi, 0, 2), pipeline_mode=pl.Buffered(1)),
            pl.BlockSpec((None, TQ, D_ATT), lambda bi, i: (bi, i, 3)),
            pl.BlockSpec((None, TQ, LANES), lambda bi, i: (bi, i, 0)),
            pl.BlockSpec((None, seq, LANES), lambda bi, i: (bi, 0, 0)),
        ],
        out_specs=pl.BlockSpec((None, TQ, D_ATT), lambda bi, i: (bi, i, 0)),
        out_shape=jax.ShapeDtypeStruct((b, seq, D_ATT), BF16),
        scratch_shapes=[
            pltpu.VMEM((N_ATT_HEADS, seq, KX_W), BF16),
            pltpu.VMEM((N_ATT_HEADS, seq, KX_W), BF16),
            pltpu.VMEM((N_ATT_HEADS, TQ, KX_W), BF16),
            pltpu.VMEM((HEADS_PER_STEP, seq // TK, TQ, TK), F32),
            pltpu.VMEM((HEADS_PER_STEP, seq // TK, TQ, TK), BF16),
            pltpu.VMEM((HEADS_PER_STEP, TQ, KX_W), F32),
            pltpu.VMEM((N_ATT_HEADS, TQ, HEAD_DIM), F32),
        ],
        compiler_params=_cparams(("parallel", "arbitrary")),
        name="fox_attention",
    )(proj3, proj3, proj3, proj3, augq, augk)


SSD_BATCH = 2
HIST = 16
N_SHIFT = CONV_K - 1
CONV_COLS = 512
PAIR_W = 2 * SSD_HEAD_DIM


def _split2_dot(val, e2_bf16):
    hi = val.astype(BF16)
    lo = (val - hi.astype(F32)).astype(BF16)
    return jnp.dot(jnp.concatenate([hi, lo], axis=1), e2_bf16, preferred_element_type=F32)


def _ssd_kernel(z_ref, xr_ref, bcr_ref, cw_ref, cb_ref, acol_ref, arow_ref, dt_ref, e_ref, dsk_ref, nw_ref,
                y_ref, uext_ref, hist_ref, sh_ref, st_ref, xs_ref, bc_ref, ex_ref, xbd_ref, xdec_ref, yacc_ref):
    c = pl.program_id(1)
    seqs = range(SSD_BATCH)

    @pl.when(c == 0)
    def _():
        hist_ref[...] = jnp.zeros_like(hist_ref)
        st_ref[...] = jnp.zeros_like(st_ref)

    out_row = lax.broadcasted_iota(jnp.int32, (N_SHIFT * CHUNK, HIST + CHUNK), 0)
    src_row = lax.broadcasted_iota(jnp.int32, (N_SHIFT * CHUNK, HIST + CHUNK), 1)
    shift = out_row // CHUNK + 1
    shift_mat = jnp.where(src_row == HIST + out_row % CHUNK - shift, 1.0, 0.0).astype(BF16)
    for e in seqs:
        uext_ref[e, 0:HIST, :] = hist_ref[e]
        uext_ref[e, HIST:HIST + CHUNK, 0:D_SSD] = xr_ref[e]
        uext_ref[e, HIST:HIST + CHUNK, D_SSD:D_XBC] = bcr_ref[e]
        hist_ref[e, :, 0:D_SSD] = xr_ref[e, CHUNK - HIST:CHUNK, :]
        hist_ref[e, :, D_SSD:D_XBC] = bcr_ref[e, CHUNK - HIST:CHUNK, :]
    for e in seqs:
        sh_ref[e] = jnp.dot(shift_mat, uext_ref[e], preferred_element_type=F32)
    for s in range(D_XBC // CONV_COLS):
        cs = slice(s * CONV_COLS, (s + 1) * CONV_COLS)
        for e in seqs:
            u = cb_ref[:, cs] + cw_ref[CONV_K - 1:CONV_K, cs] * uext_ref[e, HIST:HIST + CHUNK, cs].astype(F32)
            for kk in range(N_SHIFT):
                blk = N_SHIFT - 1 - kk
                u = u + cw_ref[kk:kk + 1, cs] * sh_ref[e, blk * CHUNK:(blk + 1) * CHUNK, cs]
            u = u * jax.nn.sigmoid(u)
            if s * CONV_COLS < D_SSD:
                xs_ref[e, :, cs] = u
            else:
                bc_ref[e, :, s * CONV_COLS - D_SSD:(s + 1) * CONV_COLS - D_SSD] = u.astype(BF16)

    lane = lax.broadcasted_iota(jnp.int32, (CHUNK, LANES), 1)
    is_dt = jnp.logical_and(lane >= DT_LANE0, lane < DT_LANE0 + N_SSD_HEADS)
    expand = e_ref[...]
    for e in seqs:
        acs = acol_ref[e]
        a_last = acs[CHUNK - 1:CHUNK, :]
        exp_a = jnp.where(is_dt, jnp.exp2(acs), 0.0)
        dec_end = jnp.where(is_dt, jnp.exp2(a_last - acs), 0.0)
        ex_ref[e, 0] = _split2_dot(dt_ref[e], expand)
        ex_ref[e, 1] = _split2_dot(exp_a, expand)
        ex_ref[e, 2] = _split2_dot(dec_end, expand)
    first_head = lax.broadcasted_iota(jnp.int32, (CHUNK, PAIR_W), 1) < SSD_HEAD_DIM
    for e in seqs:
        xdt = xs_ref[e] * ex_ref[e, 0]
        xdec_ref[e] = (xdt * ex_ref[e, 2]).astype(BF16)
        for pp in range(N_SSD_HEADS // 2):
            pair = xdt[:, pp * PAIR_W:(pp + 1) * PAIR_W]
            xbd_ref[e, pp, 0:CHUNK, :] = jnp.where(first_head, pair, 0.0).astype(BF16)
            xbd_ref[e, pp, CHUNK:2 * CHUNK, :] = jnp.where(first_head, 0.0, pair).astype(BF16)

    rr = lax.broadcasted_iota(jnp.int32, (CHUNK, CHUNK), 0)
    cc = lax.broadcasted_iota(jnp.int32, (CHUNK, CHUNK), 1)
    causal = rr >= cc
    for g in range(SSD_GROUPS):
        gs = slice(g * GROUP_W, (g + 1) * GROUP_W)
        for e in seqs:
            acs = acol_ref[e]
            arow = arow_ref[e]
            bg = bc_ref[e, :, g * D_STATE:(g + 1) * D_STATE]
            cg = bc_ref[e, :, D_BC + g * D_STATE:D_BC + (g + 1) * D_STATE]
            cb = lax.dot_general(cg, bg, NT_DIMS, preferred_element_type=F32)
            st = st_ref[e, g]
            y_off = jnp.dot(cg, st.astype(BF16), preferred_element_type=F32) * ex_ref[e, 1, :, gs]
            yacc_ref[e, :, gs] = y_off
            for pp in range(g * HEADS_PER_GROUP // 2, (g + 1) * HEADS_PER_GROUP // 2):
                mats = []
                for h in (2 * pp, 2 * pp + 1):
                    col = acs[:, DT_LANE0 + h:DT_LANE0 + h + 1]
                    row = arow[DT_LANE0 + h:DT_LANE0 + h + 1, :]
                    decay = jnp.exp2(jnp.where(causal, col - row, -jnp.inf))
                    mats.append((cb * decay).astype(BF16))
                psl = slice(pp * PAIR_W, (pp + 1) * PAIR_W)
                yacc_ref[e, :, psl] += jnp.dot(jnp.concatenate(mats, axis=1), xbd_ref[e, pp],
                                               preferred_element_type=F32)
            upd = lax.dot_general(bg, xdec_ref[e, :, gs], (((0,), (0,)), ((), ())), preferred_element_type=F32)
            st_ref[e, g] = st * ex_ref[e, 1, CHUNK - 1:CHUNK, gs] + upd

    for g in range(SSD_GROUPS):
        gs = slice(g * GROUP_W, (g + 1) * GROUP_W)
        for e in seqs:
            z = z_ref[e, :, gs].astype(F32)
            u = (yacc_ref[e, :, gs] + dsk_ref[:, gs] * xs_ref[e, :, gs]) * (z * jax.nn.sigmoid(z))
            ms = jnp.mean(u * u, axis=-1, keepdims=True)
            y_ref[e, :, gs] = (u * lax.rsqrt(ms + NORM_EPS) * nw_ref[:, gs]).astype(BF16)


def _ssd(proj3, conv_w, conv_b, acol, arow, dtv, expand, dskip_e, norm_w):
    b, seq, _ = proj3.shape
    nc = seq // CHUNK
    nb = SSD_BATCH
    zb = (3 * D_ATT + D_ATT) // D_SSD
    return pl.pallas_call(
        _ssd_kernel,
        grid=(b // nb, nc),
        in_specs=[
            pl.BlockSpec((nb, CHUNK, D_SSD), lambda bi, ci: (bi, ci, zb)),
            pl.BlockSpec((nb, CHUNK, D_SSD), lambda bi, ci: (bi, ci, zb + 1)),
            pl.BlockSpec((nb, CHUNK, D_SSD), lambda bi, ci: (bi, ci, zb + 2)),
            pl.BlockSpec((CONV_K, D_XBC), lambda bi, ci: (0, 0)),
            pl.BlockSpec((1, D_XBC), lambda bi, ci: (0, 0)),
            pl.BlockSpec((nb, CHUNK, LANES), lambda bi, ci: (bi, ci, 0)),
            pl.BlockSpec((nb, None, CHUNK, LANES), lambda bi, ci: (bi, ci, 0, 0)),
            pl.BlockSpec((nb, CHUNK, LANES), lambda bi, ci: (bi, ci, 0)),
            pl.BlockSpec((2 * LANES, D_SSD), lambda bi, ci: (0, 0)),
            pl.BlockSpec((1, D_SSD), lambda bi, ci: (0, 0)),
            pl.BlockSpec((1, D_SSD), lambda bi, ci: (0, 0)),
        ],
        out_specs=pl.BlockSpec((nb, CHUNK, D_SSD), lambda bi, ci: (bi, ci, 0)),
        out_shape=jax.ShapeDtypeStruct((b, seq, D_SSD), BF16),
        scratch_shapes=[
            pltpu.VMEM((nb, HIST + CHUNK, D_XBC), BF16),
            pltpu.VMEM((nb, HIST, D_XBC), BF16),
            pltpu.VMEM((nb, N_SHIFT * CHUNK, D_XBC), F32),
            pltpu.VMEM((nb, SSD_GROUPS, D_STATE, GROUP_W), F32),
            pltpu.VMEM((nb, CHUNK, D_SSD), F32),
            pltpu.VMEM((nb, CHUNK, 2 * D_BC), BF16),
            pltpu.VMEM((nb, 3, CHUNK, D_SSD), F32),
            pltpu.VMEM((nb, N_SSD_HEADS // 2, 2 * CHUNK, PAIR_W), BF16),
            pltpu.VMEM((nb, CHUNK, D_SSD), BF16),
            pltpu.VMEM((nb, CHUNK, D_SSD), F32),
        ],
        compiler_params=_cparams(("parallel", "arbitrary")),
        name="ssd",
    )(proj3, proj3, proj3, conv_w, conv_b, acol, arow, dtv, expand, dskip_e, norm_w)


TM_OUT = 512


def _out_kernel(att_ref, y_ref, w_ref, x_ref, g_ref, b_ref, o_ref, *, alpha):
    h = jnp.dot(att_ref[...], w_ref[0:D_ATT, :], preferred_element_type=F32)
    h = h + jnp.dot(y_ref[...], w_ref[D_ATT:D_MIX, :], preferred_element_type=F32)
    r = alpha * x_ref[...] + h
    mu = jnp.mean(r, axis=-1, keepdims=True)
    d = r - mu
    var = jnp.mean(d * d, axis=-1, keepdims=True)
    o_ref[...] = d * lax.rsqrt(var + NORM_EPS) * g_ref[...] + b_ref[...]


def _outproj(att2, y2, w_out, x2, ln_g, ln_b, alpha):
    m = x2.shape[0]
    return pl.pallas_call(
        functools.partial(_out_kernel, alpha=alpha),
        grid=(m // TM_OUT,),
        in_specs=[
            pl.BlockSpec((TM_OUT, D_ATT), lambda i: (i, 0)),
            pl.BlockSpec((TM_OUT, D_SSD), lambda i: (i, 0)),
            pl.BlockSpec((D_MIX, D_MODEL), lambda i: (0, 0), pipeline_mode=pl.Buffered(1)),
            pl.BlockSpec((TM_OUT, D_MODEL), lambda i: (i, 0)),
            pl.BlockSpec((1, D_MODEL), lambda i: (0, 0)),
            pl.BlockSpec((1, D_MODEL), lambda i: (0, 0)),
        ],
        out_specs=pl.BlockSpec((TM_OUT, D_MODEL), lambda i: (i, 0)),
        out_shape=jax.ShapeDtypeStruct((m, D_MODEL), F32),
        compiler_params=_cparams(("parallel",)),
        name="outproj_deepnorm",
    )(att2, y2, w_out, x2, ln_g, ln_b)


def _pad_lanes(vec, lane0):
    out = jnp.zeros((1, LANES), F32)
    return out.at[0, lane0:lane0 + vec.shape[0]].set(vec.astype(F32))


def _layer(x, w_in, b_forget, conv_w, conv_b, dt_bias, a_log, d_skip, ssd_norm_w, w_out, ln_g, ln_b, alpha):
    b, seq, _ = x.shape
    x2 = x.reshape(b * seq, D_MODEL)

    w_main_t, w_small_t = _wprep(w_in.T)
    bias_vec = _pad_lanes(b_forget, F_LANE0) + _pad_lanes(dt_bias, DT_LANE0)
    alog_vec = _pad_lanes(a_log, DT_LANE0)
    heads = jnp.arange(D_SSD, dtype=jnp.int32) // SSD_HEAD_DIM
    expand = (jnp.arange(LANES, dtype=jnp.int32)[:, None] == heads[None, :] + DT_LANE0).astype(BF16)
    expand = jnp.concatenate([expand, expand], axis=0)
    dskip_e = jnp.repeat(d_skip.astype(F32), SSD_HEAD_DIM)[None, :]

    proj, small = _inproj(x2, w_main_t, w_small_t)
    proj3 = proj.reshape(b, seq, D_MAIN)
    gcol, dtv, grow, augq, augk = _gates(small.reshape(b, seq, LANES), bias_vec, alog_vec)

    att = _attention(proj3, augq, augk)
    y = _ssd(proj3, conv_w.astype(F32), conv_b.astype(F32)[None, :], gcol, grow, dtv, expand, dskip_e,
             ssd_norm_w.astype(F32)[None, :])
    out = _outproj(att.reshape(b * seq, D_ATT), y.reshape(b * seq, D_SSD), w_out.astype(BF16), x2,
                   ln_g.astype(F32)[None, :], ln_b.astype(F32)[None, :], alpha)
    return out.reshape(b, seq, D_MODEL)


def kernel(x, w_in, b_forget, conv_w, conv_b, dt_bias, a_log, d_skip, ssd_norm_w, w_out, ln_g, ln_b):
    depth = w_in.shape[0]
    alpha = (2.0 * depth) ** 0.25
    for i in range(depth):
        x = _layer(x, w_in[i], b_forget[i], conv_w[i], conv_b[i], dt_bias[i], a_log[i], d_skip[i],
                   ssd_norm_w[i], w_out[i], ln_g[i], ln_b[i], alpha)
    return x
```

```python
import functools
import math

import jax
import jax.numpy as jnp
from jax import lax
from jax.experimental import pallas as pl
from jax.experimental.pallas import tpu as pltpu

F32 = jnp.float32
BF16 = jnp.bfloat16

D_MODEL = 2048
D_ATT = 1024
HEAD_DIM = 128
N_ATT_HEADS = 8
D_SSD = 2048
SSD_HEAD_DIM = 64
N_SSD_HEADS = 32
SSD_GROUPS = 8
HEADS_PER_GROUP = N_SSD_HEADS // SSD_GROUPS
D_STATE = 128
CONV_K = 4
CHUNK = 128
D_BC = SSD_GROUPS * D_STATE
D_XBC = D_SSD + 2 * D_BC
D_MIX = D_ATT + D_SSD
NORM_EPS = 1e-5
LANES = 128
GROUP_W = HEADS_PER_GROUP * SSD_HEAD_DIM

_OFF_F = 3 * D_ATT
_OFF_ZATT = _OFF_F + N_ATT_HEADS
_OFF_DT = _OFF_ZATT + D_ATT + D_SSD + D_XBC
D_MAIN = 3 * D_ATT + D_ATT + D_SSD + D_XBC
F_LANE0 = 0
DT_LANE0 = N_ATT_HEADS

AUG_PIECES = 3
AUG_Q0 = AUG_PIECES * N_ATT_HEADS
LOG2E = math.log2(math.e)

VMEM_LIMIT = 56 * 1024 * 1024


def _cparams(sem, **kw):
    return pltpu.CompilerParams(dimension_semantics=sem, vmem_limit_bytes=VMEM_LIMIT, **kw)


TK_PREP = 256
ATT_SCALE = 1.0 / math.sqrt(HEAD_DIM)
NT_DIMS = (((1,), (1,)), ((), ()))


def _wprep_kernel(wt_ref, wm_ref, ws_ref):
    wm_ref[0:D_ATT, :] = (wt_ref[0:D_ATT, :] * (ATT_SCALE * LOG2E)).astype(BF16)
    wm_ref[D_ATT:_OFF_F, :] = wt_ref[D_ATT:_OFF_F, :].astype(BF16)
    wm_ref[_OFF_F:D_MAIN, :] = wt_ref[_OFF_ZATT:_OFF_DT, :].astype(BF16)
    small = jnp.concatenate(
        [wt_ref[_OFF_F:_OFF_ZATT, :], wt_ref[_OFF_DT:_OFF_DT + N_SSD_HEADS, :],
         jnp.zeros((LANES - N_ATT_HEADS - N_SSD_HEADS, TK_PREP), F32)], axis=0)
    ws_ref[...] = small.astype(BF16)


def _wprep(w_t):
    n, k = w_t.shape
    return pl.pallas_call(
        _wprep_kernel,
        grid=(k // TK_PREP,),
        in_specs=[pl.BlockSpec((n, TK_PREP), lambda i: (0, i))],
        out_specs=[pl.BlockSpec((D_MAIN, TK_PREP), lambda i: (0, i)),
                   pl.BlockSpec((LANES, TK_PREP), lambda i: (0, i))],
        out_shape=[jax.ShapeDtypeStruct((D_MAIN, k), BF16), jax.ShapeDtypeStruct((LANES, k), BF16)],
        compiler_params=_cparams(("parallel",)),
        name="weight_prep",
    )(w_t)


TM_IN = 1024
TN_IN = 2048


def _inproj_kernel(x_ref, w_ref, ws_ref, proj_ref, small_ref, xb_ref):
    @pl.when(pl.program_id(1) == 0)
    def _():
        xb = x_ref[...].astype(BF16)
        xb_ref[...] = xb
        small_ref[...] = lax.dot_general(xb, ws_ref[...], NT_DIMS, preferred_element_type=F32)

    proj_ref[...] = lax.dot_general(xb_ref[...], w_ref[...], NT_DIMS, preferred_element_type=F32).astype(BF16)


def _inproj(x2, w_main_t, w_small_t):
    m = x2.shape[0]
    return pl.pallas_call(
        _inproj_kernel,
        grid=(m // TM_IN, D_MAIN // TN_IN),
        in_specs=[
            pl.BlockSpec((TM_IN, D_MODEL), lambda i, j: (i, 0)),
            pl.BlockSpec((TN_IN, D_MODEL), lambda i, j: (j, 0)),
            pl.BlockSpec((LANES, D_MODEL), lambda i, j: (0, 0)),
        ],
        out_specs=[
            pl.BlockSpec((TM_IN, TN_IN), lambda i, j: (i, j)),
            pl.BlockSpec((TM_IN, LANES), lambda i, j: (i, 0)),
        ],
        out_shape=[
            jax.ShapeDtypeStruct((m, D_MAIN), BF16),
            jax.ShapeDtypeStruct((m, LANES), F32),
        ],
        scratch_shapes=[pltpu.VMEM((TM_IN, D_MODEL), BF16)],
        compiler_params=_cparams(("parallel", "arbitrary")),
        name="inproj",
    )(x2, w_main_t, w_small_t)


def _split_dot(t_bf16, val):
    hi = val.astype(BF16)
    r1 = val - hi.astype(F32)
    mid = r1.astype(BF16)
    lo = (r1 - mid.astype(F32)).astype(BF16)
    return (jnp.dot(t_bf16, hi, preferred_element_type=F32)
            + jnp.dot(t_bf16, mid, preferred_element_type=F32)
            + jnp.dot(t_bf16, lo, preferred_element_type=F32))


def _split3(val):
    hi = val.astype(BF16)
    r1 = val - hi.astype(F32)
    mid = r1.astype(BF16)
    lo = (r1 - mid.astype(F32)).astype(BF16)
    return hi, mid, lo


def _gate_kernel(small_ref, bias_ref, alog_ref, col_ref, dt_ref, row_ref, augq_ref, augk_ref):
    seq = small_ref.shape[0]
    lane = lax.broadcasted_iota(jnp.int32, (CHUNK, LANES), 1)
    is_f = lane < DT_LANE0
    is_dt = jnp.logical_and(lane >= DT_LANE0, lane < DT_LANE0 + N_SSD_HEADS)
    r = lax.broadcasted_iota(jnp.int32, (CHUNK, CHUNK), 0)
    c = lax.broadcasted_iota(jnp.int32, (CHUNK, CHUNK), 1)
    tri = jnp.where(r >= c, 1.0, 0.0).astype(BF16)
    is_head = r < N_ATT_HEADS
    place_k = [jnp.where(jnp.logical_and(is_head, c == AUG_PIECES * r + jj), 1.0, 0.0).astype(BF16)
               for jj in range(AUG_PIECES)]
    place_q = [jnp.where(jnp.logical_and(is_head, c == AUG_Q0 + AUG_PIECES * r + jj), 1.0, 0.0).astype(BF16)
               for jj in range(AUG_PIECES)]
    a_neg = -jnp.exp(alog_ref[...])
    bias = bias_ref[...]
    carry = jnp.zeros((1, LANES), F32)
    for ci in range(seq // CHUNK):
        rows = slice(ci * CHUNK, (ci + 1) * CHUNK)
        v = small_ref[rows, :] + bias
        t = jnp.log1p(jnp.exp(-jnp.abs(v)))
        log_f = jnp.minimum(v, 0.0) - t
        dt = jnp.maximum(v, 0.0) + t
        val = jnp.where(is_f, log_f, jnp.where(is_dt, dt * a_neg, 0.0)) * LOG2E
        out = _split_dot(tri, val) + carry
        col_ref[rows, :] = out
        dt_ref[rows, :] = jnp.where(is_dt, dt, 0.0)
        row_ref[ci] = out.T
        pieces = _split3(out)
        acc_k = sum(jnp.dot(p, m, preferred_element_type=F32) for p, m in zip(pieces, place_k))
        acc_q = sum(jnp.dot(p, m, preferred_element_type=F32) for p, m in zip(pieces, place_q))
        in_k = lane < AUG_Q0
        in_q = jnp.logical_and(lane >= AUG_Q0, lane < 2 * AUG_Q0)
        augk_ref[rows, :] = jnp.where(in_k, -acc_k, jnp.where(in_q, 1.0, 0.0)).astype(BF16)
        augq_ref[rows, :] = jnp.where(in_k, 1.0, acc_q).astype(BF16)
        carry = jnp.where(is_f[:1], out[CHUNK - 1:CHUNK, :], 0.0)


def _gates(small3, bias_vec, alog_vec):
    b, seq, _ = small3.shape
    nc = seq // CHUNK
    return pl.pallas_call(
        _gate_kernel,
        grid=(b,),
        in_specs=[
            pl.BlockSpec((None, seq, LANES), lambda i: (i, 0, 0)),
            pl.BlockSpec((1, LANES), lambda i: (0, 0)),
            pl.BlockSpec((1, LANES), lambda i: (0, 0)),
        ],
        out_specs=[
            pl.BlockSpec((None, seq, LANES), lambda i: (i, 0, 0)),
            pl.BlockSpec((None, seq, LANES), lambda i: (i, 0, 0)),
            pl.BlockSpec((None, nc, CHUNK, LANES), lambda i: (i, 0, 0, 0)),
            pl.BlockSpec((None, seq, LANES), lambda i: (i, 0, 0)),
            pl.BlockSpec((None, seq, LANES), lambda i: (i, 0, 0)),
        ],
        out_shape=[
            jax.ShapeDtypeStruct((b, seq, LANES), F32),
            jax.ShapeDtypeStruct((b, seq, LANES), F32),
            jax.ShapeDtypeStruct((b, nc, CHUNK, LANES), F32),
            jax.ShapeDtypeStruct((b, seq, LANES), BF16),
            jax.ShapeDtypeStruct((b, seq, LANES), BF16),
        ],
        compiler_params=_cparams(("parallel",)),
        name="gates",
    )(small3, bias_vec, alog_vec)


TQ = 512
TK = 512
HEADS_PER_STEP = 2
BUFFER_SETS = 2
KX_W = HEAD_DIM + LANES


def _attn_kernel(q_ref, k_ref, v_ref, z_ref, aq_ref, ak_ref, o_ref,
                 qx_ref, s_ref, p_ref, acc_ref, o_scr):
    i = pl.program_id(1)
    seq = k_ref.shape[0]

    def kx(h, keys):
        return jnp.concatenate([k_ref[keys, h * HEAD_DIM:(h + 1) * HEAD_DIM], ak_ref[keys, :]], axis=1)

    def vx(h, keys):
        ones = jnp.ones((keys.stop - keys.start, LANES), BF16)
        return jnp.concatenate([v_ref[keys, h * HEAD_DIM:(h + 1) * HEAD_DIM], ones], axis=1)

    lane = lax.broadcasted_iota(jnp.int32, (1, LANES), 1)
    aq = aq_ref[...].astype(F32)
    for h in range(N_ATT_HEADS):
        k_lo = AUG_PIECES * h
        q_lo = AUG_Q0 + AUG_PIECES * h
        own = jnp.logical_or(jnp.logical_and(lane >= k_lo, lane < k_lo + AUG_PIECES),
                             jnp.logical_and(lane >= q_lo, lane < q_lo + AUG_PIECES))
        qx_ref[h, :, 0:HEAD_DIM] = q_ref[:, h * HEAD_DIM:(h + 1) * HEAD_DIM]
        qx_ref[h, :, HEAD_DIM:KX_W] = (aq * jnp.where(own, 1.0, 0.0)).astype(BF16)

    top_rows, bot_rows = slice(0, TQ // 2), slice(TQ // 2, TQ)
    top_keys, bot_keys = slice(0, TK // 2), slice(0, TK)
    rt = lax.broadcasted_iota(jnp.int32, (TQ // 2, TK // 2), 0)
    ct = lax.broadcasted_iota(jnp.int32, (TQ // 2, TK // 2), 1)
    rb = lax.broadcasted_iota(jnp.int32, (TQ // 2, TK), 0)
    cb = lax.broadcasted_iota(jnp.int32, (TQ // 2, TK), 1)
    diag_parts = ((top_rows, top_keys, rt >= ct), (bot_rows, bot_keys, rb + TQ // 2 >= cb))

    def head_group(g, n_blocks):
        heads = [g * HEADS_PER_STEP + hh for hh in range(HEADS_PER_STEP)]
        slot = [(g % BUFFER_SETS) * HEADS_PER_STEP + hh for hh in range(HEADS_PER_STEP)]
        last = n_blocks - 1

        def qk(hh, j):
            h = heads[hh]
            if j < last:
                s = lax.dot_general(qx_ref[h], kx(h, slice(j * TK, (j + 1) * TK)), NT_DIMS,
                                    preferred_element_type=F32)
                s_ref[slot[hh], j] = s
                return jnp.max(s, axis=1, keepdims=True)
            maxes = []
            for rows, keys, visible in diag_parts:
                s = lax.dot_general(qx_ref[h, rows, :], kx(h, slice(j * TK + keys.start, j * TK + keys.stop)),
                                    NT_DIMS, preferred_element_type=F32)
                s = jnp.where(visible, s, -jnp.inf)
                s_ref[slot[hh], j, rows, keys] = s
                maxes.append(jnp.max(s, axis=1, keepdims=True))
            return maxes

        m = [None] * HEADS_PER_STEP
        blk_max = [qk(hh, 0) for hh in range(HEADS_PER_STEP)]
        for j in range(n_blocks):
            nxt_max = [qk(hh, j + 1) for hh in range(HEADS_PER_STEP)] if j + 1 < n_blocks else None
            for hh in range(HEADS_PER_STEP):
                h, sl = heads[hh], slot[hh]
                if j < last:
                    m_new = blk_max[hh] if j == 0 else jnp.maximum(m[hh], blk_max[hh])
                    p_ref[sl, j] = jnp.exp2(s_ref[sl, j] - m_new).astype(BF16)
                    pv = jnp.dot(p_ref[sl, j], vx(h, slice(j * TK, (j + 1) * TK)), preferred_element_type=F32)
                    if j == 0:
                        acc_ref[sl] = pv
                    else:
                        acc_ref[sl] = jnp.exp2(m[hh] - m_new) * acc_ref[sl] + pv
                    m[hh] = m_new
                    continue
                for (rows, keys, _), part_max in zip(diag_parts, blk_max[hh]):
                    m_old = None if j == 0 else m[hh][rows]
                    m_new = part_max if j == 0 else jnp.maximum(m_old, part_max)
                    p_ref[sl, j, rows, keys] = jnp.exp2(s_ref[sl, j, rows, keys] - m_new).astype(BF16)
                    pv = jnp.dot(p_ref[sl, j, rows, keys], vx(h, slice(j * TK + keys.start, j * TK + keys.stop)),
                                 preferred_element_type=F32)
                    if j == 0:
                        acc_ref[sl, rows] = pv
                    else:
                        acc_ref[sl, rows] = jnp.exp2(m_old - m_new) * acc_ref[sl, rows] + pv
            blk_max = nxt_max
        for hh in range(HEADS_PER_STEP):
            acc = acc_ref[slot[hh]]
            o_scr[heads[hh]] = acc[:, 0:HEAD_DIM] / acc[:, HEAD_DIM:KX_W]

    for n_blocks in range(1, seq // TQ + 1):
        for g in range(N_ATT_HEADS // HEADS_PER_STEP):
            @pl.when(i == n_blocks - 1)
            def _(n_blocks=n_blocks, g=g):
                head_group(g, n_blocks)

    for h in range(N_ATT_HEADS):
        hs = slice(h * HEAD_DIM, (h + 1) * HEAD_DIM)
        z = z_ref[:, hs].astype(F32)
        o_ref[:, hs] = (o_scr[h] * (z * jax.nn.sigmoid(z))).astype(BF16)


def _attention(proj3, augq, augk):
    b, seq, _ = proj3.shape
    return pl.pallas_call(
        _attn_kernel,
        grid=(b, seq // TQ),
        in_specs=[
            pl.BlockSpec((None, TQ, D_ATT), lambda bi, i: (bi, i, 0)),
            pl.BlockSpec((None, seq, D_ATT), lambda bi, i: (bi, 0, 1)),
            pl.BlockSpec((None, seq, D_ATT), lambda bi, i: (bi, 0, 2)),
            pl.BlockSpec((None, TQ, D_ATT), lambda bi, i: (bi, i, 3)),
            pl.BlockSpec((None, TQ, LANES), lambda bi, i: (bi, i, 0)),
            pl.BlockSpec((None, seq, LANES), lambda bi, i: (bi, 0, 0)),
        ],
        out_specs=pl.BlockSpec((None, TQ, D_ATT), lambda bi, i: (bi, i, 0)),
        out_shape=jax.ShapeDtypeStruct((b, seq, D_ATT), BF16),
        scratch_shapes=[
            pltpu.VMEM((N_ATT_HEADS, TQ, KX_W), BF16),
            pltpu.VMEM((BUFFER_SETS * HEADS_PER_STEP, seq // TK, TQ, TK), F32),
            pltpu.VMEM((BUFFER_SETS * HEADS_PER_STEP, seq // TK, TQ, TK), BF16),
            pltpu.VMEM((BUFFER_SETS * HEADS_PER_STEP, TQ, KX_W), F32),
            pltpu.VMEM((N_ATT_HEADS, TQ, HEAD_DIM), F32),
        ],
        compiler_params=_cparams(("parallel", "arbitrary")),
        name="fox_attention",
    )(proj3, proj3, proj3, proj3, augq, augk)


SSD_BATCH = 2
HIST = 16
N_SHIFT = CONV_K - 1
CONV_COLS = 512
PAIR_W = 2 * SSD_HEAD_DIM


def _split2_dot(val, e2_bf16):
    hi = val.astype(BF16)
    lo = (val - hi.astype(F32)).astype(BF16)
    return jnp.dot(jnp.concatenate([hi, lo], axis=1), e2_bf16, preferred_element_type=F32)


def _ssd_kernel(z_ref, xr_ref, bcr_ref, cw_ref, cb_ref, acol_ref, arow_ref, dt_ref, e_ref, dsk_ref, nw_ref,
                y_ref, uext_ref, hist_ref, sh_ref, st_ref, xs_ref, bc_ref, ex_ref, xbd_ref, xdec_ref, yacc_ref):
    c = pl.program_id(1)
    seqs = range(SSD_BATCH)

    @pl.when(c == 0)
    def _():
        hist_ref[...] = jnp.zeros_like(hist_ref)
        st_ref[...] = jnp.zeros_like(st_ref)

    out_row = lax.broadcasted_iota(jnp.int32, (N_SHIFT * CHUNK, HIST + CHUNK), 0)
    src_row = lax.broadcasted_iota(jnp.int32, (N_SHIFT * CHUNK, HIST + CHUNK), 1)
    shift = out_row // CHUNK + 1
    shift_mat = jnp.where(src_row == HIST + out_row % CHUNK - shift, 1.0, 0.0).astype(BF16)
    for e in seqs:
        uext_ref[e, 0:HIST, :] = hist_ref[e]
        uext_ref[e, HIST:HIST + CHUNK, 0:D_SSD] = xr_ref[e]
        uext_ref[e, HIST:HIST + CHUNK, D_SSD:D_XBC] = bcr_ref[e]
        hist_ref[e, :, 0:D_SSD] = xr_ref[e, CHUNK - HIST:CHUNK, :]
        hist_ref[e, :, D_SSD:D_XBC] = bcr_ref[e, CHUNK - HIST:CHUNK, :]
    for e in seqs:
        sh_ref[e] = jnp.dot(shift_mat, uext_ref[e], preferred_element_type=F32)
    for s in range(D_XBC // CONV_COLS):
        cs = slice(s * CONV_COLS, (s + 1) * CONV_COLS)
        for e in seqs:
            u = cb_ref[:, cs] + cw_ref[CONV_K - 1:CONV_K, cs] * uext_ref[e, HIST:HIST + CHUNK, cs].astype(F32)
            for kk in range(N_SHIFT):
                blk = N_SHIFT - 1 - kk
                u = u + cw_ref[kk:kk + 1, cs] * sh_ref[e, blk * CHUNK:(blk + 1) * CHUNK, cs]
            u = u * jax.nn.sigmoid(u)
            if s * CONV_COLS < D_SSD:
                xs_ref[e, :, cs] = u
            else:
                bc_ref[e, :, s * CONV_COLS - D_SSD:(s + 1) * CONV_COLS - D_SSD] = u.astype(BF16)

    lane = lax.broadcasted_iota(jnp.int32, (CHUNK, LANES), 1)
    is_dt = jnp.logical_and(lane >= DT_LANE0, lane < DT_LANE0 + N_SSD_HEADS)
    expand = e_ref[...]
    for e in seqs:
        acs = acol_ref[e]
        a_last = acs[CHUNK - 1:CHUNK, :]
        exp_a = jnp.where(is_dt, jnp.exp2(acs), 0.0)
        dec_end = jnp.where(is_dt, jnp.exp2(a_last - acs), 0.0)
        ex_ref[e, 0] = _split2_dot(dt_ref[e], expand)
        ex_ref[e, 1] = _split2_dot(exp_a, expand)
        ex_ref[e, 2] = _split2_dot(dec_end, expand)
    first_head = lax.broadcasted_iota(jnp.int32, (CHUNK, PAIR_W), 1) < SSD_HEAD_DIM
    for e in seqs:
        xdt = xs_ref[e] * ex_ref[e, 0]
        xdec_ref[e] = (xdt * ex_ref[e, 2]).astype(BF16)
        for pp in range(N_SSD_HEADS // 2):
            pair = xdt[:, pp * PAIR_W:(pp + 1) * PAIR_W]
            xbd_ref[e, pp, 0:CHUNK, :] = jnp.where(first_head, pair, 0.0).astype(BF16)
            xbd_ref[e, pp, CHUNK:2 * CHUNK, :] = jnp.where(first_head, 0.0, pair).astype(BF16)

    rr = lax.broadcasted_iota(jnp.int32, (CHUNK, CHUNK), 0)
    cc = lax.broadcasted_iota(jnp.int32, (CHUNK, CHUNK), 1)
    causal = rr >= cc
    for g in range(SSD_GROUPS):
        gs = slice(g * GROUP_W, (g + 1) * GROUP_W)
        for e in seqs:
            acs = acol_ref[e]
            arow = arow_ref[e]
            bg = bc_ref[e, :, g * D_STATE:(g + 1) * D_STATE]
            cg = bc_ref[e, :, D_BC + g * D_STATE:D_BC + (g + 1) * D_STATE]
            cb = lax.dot_general(cg, bg, NT_DIMS, preferred_element_type=F32)
            st = st_ref[e, g]
            y_off = jnp.dot(cg, st.astype(BF16), preferred_element_type=F32) * ex_ref[e, 1, :, gs]
            yacc_ref[e, :, gs] = y_off
            for pp in range(g * HEADS_PER_GROUP // 2, (g + 1) * HEADS_PER_GROUP // 2):
                mats = []
                for h in (2 * pp, 2 * pp + 1):
                    col = acs[:, DT_LANE0 + h:DT_LANE0 + h + 1]
                    row = arow[DT_LANE0 + h:DT_LANE0 + h + 1, :]
                    decay = jnp.exp2(jnp.where(causal, col - row, -jnp.inf))
                    mats.append((cb * decay).astype(BF16))
                psl = slice(pp * PAIR_W, (pp + 1) * PAIR_W)
                yacc_ref[e, :, psl] += jnp.dot(jnp.concatenate(mats, axis=1), xbd_ref[e, pp],
                                               preferred_element_type=F32)
            upd = lax.dot_general(bg, xdec_ref[e, :, gs], (((0,), (0,)), ((), ())), preferred_element_type=F32)
            st_ref[e, g] = st * ex_ref[e, 1, CHUNK - 1:CHUNK, gs] + upd

    for g in range(SSD_GROUPS):
        gs = slice(g * GROUP_W, (g + 1) * GROUP_W)
        for e in seqs:
            z = z_ref[e, :, gs].astype(F32)
            u = (yacc_ref[e, :, gs] + dsk_ref[:, gs] * xs_ref[e, :, gs]) * (z * jax.nn.sigmoid(z))
            ms = jnp.mean(u * u, axis=-1, keepdims=True)
            y_ref[e, :, gs] = (u * lax.rsqrt(ms + NORM_EPS) * nw_ref[:, gs]).astype(BF16)


def _ssd(proj3, conv_w, conv_b, acol, arow, dtv, expand, dskip_e, norm_w):
    b, seq, _ = proj3.shape
    nc = seq // CHUNK
    nb = SSD_BATCH
    zb = (3 * D_ATT + D_ATT) // D_SSD
    return pl.pallas_call(
        _ssd_kernel,
        grid=(b // nb, nc),
        in_specs=[
            pl.BlockSpec((nb, CHUNK, D_SSD), lambda bi, ci: (bi, ci, zb)),
            pl.BlockSpec((nb, CHUNK, D_SSD), lambda bi, ci: (bi, ci, zb + 1)),
            pl.BlockSpec((nb, CHUNK, D_SSD), lambda bi, ci: (bi, ci, zb + 2)),
            pl.BlockSpec((CONV_K, D_XBC), lambda bi, ci: (0, 0)),
            pl.BlockSpec((1, D_XBC), lambda bi, ci: (0, 0)),
            pl.BlockSpec((nb, CHUNK, LANES), lambda bi, ci: (bi, ci, 0)),
            pl.BlockSpec((nb, None, CHUNK, LANES), lambda bi, ci: (bi, ci, 0, 0)),
            pl.BlockSpec((nb, CHUNK, LANES), lambda bi, ci: (bi, ci, 0)),
            pl.BlockSpec((2 * LANES, D_SSD), lambda bi, ci: (0, 0)),
            pl.BlockSpec((1, D_SSD), lambda bi, ci: (0, 0)),
            pl.BlockSpec((1, D_SSD), lambda bi, ci: (0, 0)),
        ],
        out_specs=pl.BlockSpec((nb, CHUNK, D_SSD), lambda bi, ci: (bi, ci, 0)),
        out_shape=jax.ShapeDtypeStruct((b, seq, D_SSD), BF16),
        scratch_shapes=[
            pltpu.VMEM((nb, HIST + CHUNK, D_XBC), BF16),
            pltpu.VMEM((nb, HIST, D_XBC), BF16),
            pltpu.VMEM((nb, N_SHIFT * CHUNK, D_XBC), F32),
            pltpu.VMEM((nb, SSD_GROUPS, D_STATE, GROUP_W), F32),
            pltpu.VMEM((nb, CHUNK, D_SSD), F32),
            pltpu.VMEM((nb, CHUNK, 2 * D_BC), BF16),
            pltpu.VMEM((nb, 3, CHUNK, D_SSD), F32),
            pltpu.VMEM((nb, N_SSD_HEADS // 2, 2 * CHUNK, PAIR_W), BF16),
            pltpu.VMEM((nb, CHUNK, D_SSD), BF16),
            pltpu.VMEM((nb, CHUNK, D_SSD), F32),
        ],
        compiler_params=_cparams(("parallel", "arbitrary")),
        name="ssd",
    )(proj3, proj3, proj3, conv_w, conv_b, acol, arow, dtv, expand, dskip_e, norm_w)


TM_OUT = 512


def _out_kernel(att_ref, y_ref, w_ref, x_ref, g_ref, b_ref, o_ref, *, alpha):
    h = jnp.dot(att_ref[...], w_ref[0:D_ATT, :], preferred_element_type=F32)
    h = h + jnp.dot(y_ref[...], w_ref[D_ATT:D_MIX, :], preferred_element_type=F32)
    r = alpha * x_ref[...] + h
    mu = jnp.mean(r, axis=-1, keepdims=True)
    d = r - mu
    var = jnp.mean(d * d, axis=-1, keepdims=True)
    o_ref[...] = d * lax.rsqrt(var + NORM_EPS) * g_ref[...] + b_ref[...]


def _outproj(att2, y2, w_out, x2, ln_g, ln_b, alpha):
    m = x2.shape[0]
    return pl.pallas_call(
        functools.partial(_out_kernel, alpha=alpha),
        grid=(m // TM_OUT,),
        in_specs=[
            pl.BlockSpec((TM_OUT, D_ATT), lambda i: (i, 0)),
            pl.BlockSpec((TM_OUT, D_SSD), lambda i: (i, 0)),
            pl.BlockSpec((D_MIX, D_MODEL), lambda i: (0, 0), pipeline_mode=pl.Buffered(1)),
            pl.BlockSpec((TM_OUT, D_MODEL), lambda i: (i, 0)),
            pl.BlockSpec((1, D_MODEL), lambda i: (0, 0)),
            pl.BlockSpec((1, D_MODEL), lambda i: (0, 0)),
        ],
        out_specs=pl.BlockSpec((TM_OUT, D_MODEL), lambda i: (i, 0)),
        out_shape=jax.ShapeDtypeStruct((m, D_MODEL), F32),
        compiler_params=_cparams(("parallel",)),
        name="outproj_deepnorm",
    )(att2, y2, w_out, x2, ln_g, ln_b)


def _pad_lanes(vec, lane0):
    out = jnp.zeros((1, LANES), F32)
    return out.at[0, lane0:lane0 + vec.shape[0]].set(vec.astype(F32))


def _layer(x, w_in, b_forget, conv_w, conv_b, dt_bias, a_log, d_skip, ssd_norm_w, w_out, ln_g, ln_b, alpha):
    b, seq, _ = x.shape
    x2 = x.reshape(b * seq, D_MODEL)

    w_main_t, w_small_t = _wprep(w_in.T)
    bias_vec = _pad_lanes(b_forget, F_LANE0) + _pad_lanes(dt_bias, DT_LANE0)
    alog_vec = _pad_lanes(a_log, DT_LANE0)
    heads = jnp.arange(D_SSD, dtype=jnp.int32) // SSD_HEAD_DIM
    expand = (jnp.arange(LANES, dtype=jnp.int32)[:, None] == heads[None, :] + DT_LANE0).astype(BF16)
    expand = jnp.concatenate([expand, expand], axis=0)
    dskip_e = jnp.repeat(d_skip.astype(F32), SSD_HEAD_DIM)[None, :]

    proj, small = _inproj(x2, w_main_t, w_small_t)
    proj3 = proj.reshape(b, seq, D_MAIN)
    gcol, dtv, grow, augq, augk = _gates(small.reshape(b, seq, LANES), bias_vec, alog_vec)

    att = _attention(proj3, augq, augk)
    y = _ssd(proj3, conv_w.astype(F32), conv_b.astype(F32)[None, :], gcol, grow, dtv, expand, dskip_e,
             ssd_norm_w.astype(F32)[None, :])
    out = _outproj(att.reshape(b * seq, D_ATT), y.reshape(b * seq, D_SSD), w_out.astype(BF16), x2,
                   ln_g.astype(F32)[None, :], ln_b.astype(F32)[None, :], alpha)
    return out.reshape(b, seq, D_MODEL)


def kernel(x, w_in, b_forget, conv_w, conv_b, dt_bias, a_log, d_skip, ssd_norm_w, w_out, ln_g, ln_b):
    depth = w_in.shape[0]
    alpha = (2.0 * depth) ** 0.25
    for i in range(depth):
        x = _layer(x, w_in[i], b_forget[i], conv_w[i], conv_b[i], dt_bias[i], a_log[i], d_skip[i],
                   ssd_norm_w[i], w_out[i], ln_g[i], ln_b[i], alpha)
    return x
```

```python
import functools
import math

import jax
import jax.numpy as jnp
from jax import lax
from jax.experimental import pallas as pl
from jax.experimental.pallas import tpu as pltpu

F32 = jnp.float32
BF16 = jnp.bfloat16

D_MODEL = 2048
D_ATT = 1024
HEAD_DIM = 128
N_ATT_HEADS = 8
D_SSD = 2048
SSD_HEAD_DIM = 64
N_SSD_HEADS = 32
SSD_GROUPS = 8
HEADS_PER_GROUP = N_SSD_HEADS // SSD_GROUPS
D_STATE = 128
CONV_K = 4
CHUNK = 128
D_BC = SSD_GROUPS * D_STATE
D_XBC = D_SSD + 2 * D_BC
D_MIX = D_ATT + D_SSD
NORM_EPS = 1e-5
LANES = 128
GROUP_W = HEADS_PER_GROUP * SSD_HEAD_DIM

_OFF_F = 3 * D_ATT
_OFF_ZATT = _OFF_F + N_ATT_HEADS
_OFF_DT = _OFF_ZATT + D_ATT + D_SSD + D_XBC
D_MAIN = 3 * D_ATT + D_ATT + D_SSD + D_XBC
F_LANE0 = 0
DT_LANE0 = N_ATT_HEADS

AUG_PIECES = 3
AUG_Q0 = AUG_PIECES * N_ATT_HEADS
LOG2E = math.log2(math.e)

VMEM_LIMIT = 56 * 1024 * 1024


def _cparams(sem, **kw):
    return pltpu.CompilerParams(dimension_semantics=sem, vmem_limit_bytes=VMEM_LIMIT, **kw)


TK_PREP = 256
ATT_SCALE = 1.0 / math.sqrt(HEAD_DIM)
NT_DIMS = (((1,), (1,)), ((), ()))


def _wprep_kernel(wt_ref, wm_ref, ws_ref):
    wm_ref[0:D_ATT, :] = (wt_ref[0:D_ATT, :] * (ATT_SCALE * LOG2E)).astype(BF16)
    wm_ref[D_ATT:_OFF_F, :] = wt_ref[D_ATT:_OFF_F, :].astype(BF16)
    wm_ref[_OFF_F:D_MAIN, :] = wt_ref[_OFF_ZATT:_OFF_DT, :].astype(BF16)
    small = jnp.concatenate(
        [wt_ref[_OFF_F:_OFF_ZATT, :], wt_ref[_OFF_DT:_OFF_DT + N_SSD_HEADS, :],
         jnp.zeros((LANES - N_ATT_HEADS - N_SSD_HEADS, TK_PREP), F32)], axis=0)
    ws_ref[...] = small.astype(BF16)


def _wprep(w_t):
    n, k = w_t.shape
    return pl.pallas_call(
        _wprep_kernel,
        grid=(k // TK_PREP,),
        in_specs=[pl.BlockSpec((n, TK_PREP), lambda i: (0, i))],
        out_specs=[pl.BlockSpec((D_MAIN, TK_PREP), lambda i: (0, i)),
                   pl.BlockSpec((LANES, TK_PREP), lambda i: (0, i))],
        out_shape=[jax.ShapeDtypeStruct((D_MAIN, k), BF16), jax.ShapeDtypeStruct((LANES, k), BF16)],
        compiler_params=_cparams(("parallel",)),
        name="weight_prep",
    )(w_t)


TM_IN = 1024
TN_IN = 2048


def _inproj_kernel(x_ref, w_ref, ws_ref, proj_ref, small_ref, xb_ref):
    @pl.when(pl.program_id(1) == 0)
    def _():
        xb = x_ref[...].astype(BF16)
        xb_ref[...] = xb
        small_ref[...] = lax.dot_general(xb, ws_ref[...], NT_DIMS, preferred_element_type=F32)

    proj_ref[...] = lax.dot_general(xb_ref[...], w_ref[...], NT_DIMS, preferred_element_type=F32).astype(BF16)


def _inproj(x2, w_main_t, w_small_t):
    m = x2.shape[0]
    return pl.pallas_call(
        _inproj_kernel,
        grid=(m // TM_IN, D_MAIN // TN_IN),
        in_specs=[
            pl.BlockSpec((TM_IN, D_MODEL), lambda i, j: (i, 0)),
            pl.BlockSpec((TN_IN, D_MODEL), lambda i, j: (j, 0)),
            pl.BlockSpec((LANES, D_MODEL), lambda i, j: (0, 0)),
        ],
        out_specs=[
            pl.BlockSpec((TM_IN, TN_IN), lambda i, j: (i, j)),
            pl.BlockSpec((TM_IN, LANES), lambda i, j: (i, 0)),
        ],
        out_shape=[
            jax.ShapeDtypeStruct((m, D_MAIN), BF16),
            jax.ShapeDtypeStruct((m, LANES), F32),
        ],
        scratch_shapes=[pltpu.VMEM((TM_IN, D_MODEL), BF16)],
        compiler_params=_cparams(("parallel", "arbitrary")),
        name="inproj",
    )(x2, w_main_t, w_small_t)


def _split_dot(t_bf16, val):
    hi = val.astype(BF16)
    r1 = val - hi.astype(F32)
    mid = r1.astype(BF16)
    lo = (r1 - mid.astype(F32)).astype(BF16)
    return (jnp.dot(t_bf16, hi, preferred_element_type=F32)
            + jnp.dot(t_bf16, mid, preferred_element_type=F32)
            + jnp.dot(t_bf16, lo, preferred_element_type=F32))


def _split3(val):
    hi = val.astype(BF16)
    r1 = val - hi.astype(F32)
    mid = r1.astype(BF16)
    lo = (r1 - mid.astype(F32)).astype(BF16)
    return hi, mid, lo


def _gate_kernel(small_ref, bias_ref, alog_ref, col_ref, dt_ref, row_ref, augq_ref, augk_ref):
    seq = small_ref.shape[0]
    lane = lax.broadcasted_iota(jnp.int32, (CHUNK, LANES), 1)
    is_f = lane < DT_LANE0
    is_dt = jnp.logical_and(lane >= DT_LANE0, lane < DT_LANE0 + N_SSD_HEADS)
    r = lax.broadcasted_iota(jnp.int32, (CHUNK, CHUNK), 0)
    c = lax.broadcasted_iota(jnp.int32, (CHUNK, CHUNK), 1)
    tri = jnp.where(r >= c, 1.0, 0.0).astype(BF16)
    is_head = r < N_ATT_HEADS
    place_k = [jnp.where(jnp.logical_and(is_head, c == AUG_PIECES * r + jj), 1.0, 0.0).astype(BF16)
               for jj in range(AUG_PIECES)]
    place_q = [jnp.where(jnp.logical_and(is_head, c == AUG_Q0 + AUG_PIECES * r + jj), 1.0, 0.0).astype(BF16)
               for jj in range(AUG_PIECES)]
    a_neg = -jnp.exp(alog_ref[...])
    bias = bias_ref[...]
    for ci in range(seq // CHUNK):
        rows = slice(ci * CHUNK, (ci + 1) * CHUNK)
        v = small_ref[rows, :] + bias
        t = jnp.log1p(jnp.exp(-jnp.abs(v)))
        log_f = jnp.minimum(v, 0.0) - t
        dt = jnp.maximum(v, 0.0) + t
        val = jnp.where(is_f, log_f, jnp.where(is_dt, dt * a_neg, 0.0)) * LOG2E
        col_ref[rows, :] = _split_dot(tri, val)
        dt_ref[rows, :] = jnp.where(is_dt, dt, 0.0)
    carry = jnp.zeros((1, LANES), F32)
    for ci in range(seq // CHUNK):
        rows = slice(ci * CHUNK, (ci + 1) * CHUNK)
        out = col_ref[rows, :] + carry
        col_ref[rows, :] = out
        row_ref[ci] = out.T
        pieces = _split3(out)
        acc_k = sum(jnp.dot(p, m, preferred_element_type=F32) for p, m in zip(pieces, place_k))
        acc_q = sum(jnp.dot(p, m, preferred_element_type=F32) for p, m in zip(pieces, place_q))
        in_k = lane < AUG_Q0
        in_q = jnp.logical_and(lane >= AUG_Q0, lane < 2 * AUG_Q0)
        augk_ref[rows, :] = jnp.where(in_k, -acc_k, jnp.where(in_q, 1.0, 0.0)).astype(BF16)
        augq_ref[rows, :] = jnp.where(in_k, 1.0, acc_q).astype(BF16)
        carry = jnp.where(is_f[:1], out[CHUNK - 1:CHUNK, :], 0.0)


def _gates(small3, bias_vec, alog_vec):
    b, seq, _ = small3.shape
    nc = seq // CHUNK
    return pl.pallas_call(
        _gate_kernel,
        grid=(b,),
        in_specs=[
            pl.BlockSpec((None, seq, LANES), lambda i: (i, 0, 0)),
            pl.BlockSpec((1, LANES), lambda i: (0, 0)),
            pl.BlockSpec((1, LANES), lambda i: (0, 0)),
        ],
        out_specs=[
            pl.BlockSpec((None, seq, LANES), lambda i: (i, 0, 0)),
            pl.BlockSpec((None, seq, LANES), lambda i: (i, 0, 0)),
            pl.BlockSpec((None, nc, CHUNK, LANES), lambda i: (i, 0, 0, 0)),
            pl.BlockSpec((None, seq, LANES), lambda i: (i, 0, 0)),
            pl.BlockSpec((None, seq, LANES), lambda i: (i, 0, 0)),
        ],
        out_shape=[
            jax.ShapeDtypeStruct((b, seq, LANES), F32),
            jax.ShapeDtypeStruct((b, seq, LANES), F32),
            jax.ShapeDtypeStruct((b, nc, CHUNK, LANES), F32),
            jax.ShapeDtypeStruct((b, seq, LANES), BF16),
            jax.ShapeDtypeStruct((b, seq, LANES), BF16),
        ],
        compiler_params=_cparams(("parallel",)),
        name="gates",
    )(small3, bias_vec, alog_vec)


TQ = 512
TK = 512
HEADS_PER_STEP = 2
BUFFER_SETS = 2
KX_W = HEAD_DIM + LANES


def _attn_kernel(q_ref, k_ref, v_ref, z_ref, aq_ref, ak_ref, o_ref,
                 qx_ref, s_ref, p_ref, acc_ref):
    i = pl.program_id(1)
    seq = k_ref.shape[0]

    def kx(h, keys):
        return jnp.concatenate([k_ref[keys, h * HEAD_DIM:(h + 1) * HEAD_DIM], ak_ref[keys, :]], axis=1)

    def vx(h, keys):
        ones = jnp.ones((keys.stop - keys.start, LANES), BF16)
        return jnp.concatenate([v_ref[keys, h * HEAD_DIM:(h + 1) * HEAD_DIM], ones], axis=1)

    lane = lax.broadcasted_iota(jnp.int32, (1, LANES), 1)
    aq = aq_ref[...].astype(F32)
    for h in range(N_ATT_HEADS):
        k_lo = AUG_PIECES * h
        q_lo = AUG_Q0 + AUG_PIECES * h
        own = jnp.logical_or(jnp.logical_and(lane >= k_lo, lane < k_lo + AUG_PIECES),
                             jnp.logical_and(lane >= q_lo, lane < q_lo + AUG_PIECES))
        qx_ref[h, :, 0:HEAD_DIM] = q_ref[:, h * HEAD_DIM:(h + 1) * HEAD_DIM]
        qx_ref[h, :, HEAD_DIM:KX_W] = (aq * jnp.where(own, 1.0, 0.0)).astype(BF16)

    top_rows, bot_rows = slice(0, TQ // 2), slice(TQ // 2, TQ)
    top_keys, bot_keys = slice(0, TK // 2), slice(0, TK)
    rt = lax.broadcasted_iota(jnp.int32, (TQ // 2, TK // 2), 0)
    ct = lax.broadcasted_iota(jnp.int32, (TQ // 2, TK // 2), 1)
    rb = lax.broadcasted_iota(jnp.int32, (TQ // 2, TK), 0)
    cb = lax.broadcasted_iota(jnp.int32, (TQ // 2, TK), 1)
    diag_parts = ((top_rows, top_keys, rt >= ct), (bot_rows, bot_keys, rb + TQ // 2 >= cb))

    def head_group(g, n_blocks):
        heads = [g * HEADS_PER_STEP + hh for hh in range(HEADS_PER_STEP)]
        slot = [(g % BUFFER_SETS) * HEADS_PER_STEP + hh for hh in range(HEADS_PER_STEP)]
        last = n_blocks - 1

        def qk(hh, j):
            h = heads[hh]
            if j < last:
                s = lax.dot_general(qx_ref[h], kx(h, slice(j * TK, (j + 1) * TK)), NT_DIMS,
                                    preferred_element_type=F32)
                s_ref[slot[hh], j] = s
                return jnp.max(s, axis=1, keepdims=True)
            maxes = []
            for rows, keys, visible in diag_parts:
                s = lax.dot_general(qx_ref[h, rows, :], kx(h, slice(j * TK + keys.start, j * TK + keys.stop)),
                                    NT_DIMS, preferred_element_type=F32)
                s = jnp.where(visible, s, -jnp.inf)
                s_ref[slot[hh], j, rows, keys] = s
                maxes.append(jnp.max(s, axis=1, keepdims=True))
            return maxes

        m = [None] * HEADS_PER_STEP
        blk_max = [qk(hh, 0) for hh in range(HEADS_PER_STEP)]
        for j in range(n_blocks):
            nxt_max = [qk(hh, j + 1) for hh in range(HEADS_PER_STEP)] if j + 1 < n_blocks else None
            for hh in range(HEADS_PER_STEP):
                h, sl = heads[hh], slot[hh]
                if j < last:
                    m_new = blk_max[hh] if j == 0 else jnp.maximum(m[hh], blk_max[hh])
                    p_ref[sl, j] = jnp.exp2(s_ref[sl, j] - m_new).astype(BF16)
                    pv = jnp.dot(p_ref[sl, j], vx(h, slice(j * TK, (j + 1) * TK)), preferred_element_type=F32)
                    if j == 0:
                        acc_ref[sl] = pv
                    else:
                        acc_ref[sl] = jnp.exp2(m[hh] - m_new) * acc_ref[sl] + pv
                    m[hh] = m_new
                    continue
                for (rows, keys, _), part_max in zip(diag_parts, blk_max[hh]):
                    m_old = None if j == 0 else m[hh][rows]
                    m_new = part_max if j == 0 else jnp.maximum(m_old, part_max)
                    p_ref[sl, j, rows, keys] = jnp.exp2(s_ref[sl, j, rows, keys] - m_new).astype(BF16)
                    pv = jnp.dot(p_ref[sl, j, rows, keys], vx(h, slice(j * TK + keys.start, j * TK + keys.stop)),
                                 preferred_element_type=F32)
                    if j == 0:
                        acc_ref[sl, rows] = pv
                    else:
                        acc_ref[sl, rows] = jnp.exp2(m_old - m_new) * acc_ref[sl, rows] + pv
            blk_max = nxt_max
        for hh in range(HEADS_PER_STEP):
            hs = slice(heads[hh] * HEAD_DIM, (heads[hh] + 1) * HEAD_DIM)
            acc = acc_ref[slot[hh]]
            z = z_ref[:, hs].astype(F32)
            o_ref[:, hs] = ((acc[:, 0:HEAD_DIM] / acc[:, HEAD_DIM:KX_W]) * (z * jax.nn.sigmoid(z))).astype(BF16)

    for n_blocks in range(1, seq // TQ + 1):
        for g in range(N_ATT_HEADS // HEADS_PER_STEP):
            @pl.when(i == n_blocks - 1)
            def _(n_blocks=n_blocks, g=g):
                head_group(g, n_blocks)


def _attention(proj3, augq, augk):
    b, seq, _ = proj3.shape
    return pl.pallas_call(
        _attn_kernel,
        grid=(b, seq // TQ),
        in_specs=[
            pl.BlockSpec((None, TQ, D_ATT), lambda bi, i: (bi, i, 0)),
            pl.BlockSpec((None, seq, D_ATT), lambda bi, i: (bi, 0, 1)),
            pl.BlockSpec((None, seq, D_ATT), lambda bi, i: (bi, 0, 2)),
            pl.BlockSpec((None, TQ, D_ATT), lambda bi, i: (bi, i, 3)),
            pl.BlockSpec((None, TQ, LANES), lambda bi, i: (bi, i, 0)),
            pl.BlockSpec((None, seq, LANES), lambda bi, i: (bi, 0, 0)),
        ],
        out_specs=pl.BlockSpec((None, TQ, D_ATT), lambda bi, i: (bi, i, 0)),
        out_shape=jax.ShapeDtypeStruct((b, seq, D_ATT), BF16),
        scratch_shapes=[
            pltpu.VMEM((N_ATT_HEADS, TQ, KX_W), BF16),
            pltpu.VMEM((BUFFER_SETS * HEADS_PER_STEP, seq // TK, TQ, TK), F32),
            pltpu.VMEM((BUFFER_SETS * HEADS_PER_STEP, seq // TK, TQ, TK), BF16),
            pltpu.VMEM((BUFFER_SETS * HEADS_PER_STEP, TQ, KX_W), F32),
        ],
        compiler_params=_cparams(("parallel", "arbitrary")),
        name="fox_attention",
    )(proj3, proj3, proj3, proj3, augq, augk)


SSD_BATCH = 2
HIST = 16
N_SHIFT = CONV_K - 1
CONV_COLS = 512
PAIR_W = 2 * SSD_HEAD_DIM


def _split2_dot(val, e2_bf16):
    hi = val.astype(BF16)
    lo = (val - hi.astype(F32)).astype(BF16)
    return jnp.dot(jnp.concatenate([hi, lo], axis=1), e2_bf16, preferred_element_type=F32)


def _ssd_kernel(z_ref, xr_ref, bcr_ref, cw_ref, cb_ref, acol_ref, arow_ref, dt_ref, e_ref, dsk_ref, nw_ref,
                y_ref, uext_ref, hist_ref, sh_ref, st_ref, xs_ref, bc_ref, ex_ref, xbd_ref, xdec_ref, yacc_ref):
    c = pl.program_id(1)
    seqs = range(SSD_BATCH)

    @pl.when(c == 0)
    def _():
        hist_ref[...] = jnp.zeros_like(hist_ref)
        st_ref[...] = jnp.zeros_like(st_ref)

    out_row = lax.broadcasted_iota(jnp.int32, (N_SHIFT * CHUNK, HIST + CHUNK), 0)
    src_row = lax.broadcasted_iota(jnp.int32, (N_SHIFT * CHUNK, HIST + CHUNK), 1)
    shift = out_row // CHUNK + 1
    shift_mat = jnp.where(src_row == HIST + out_row % CHUNK - shift, 1.0, 0.0).astype(BF16)
    for e in seqs:
        uext_ref[e, 0:HIST, :] = hist_ref[e]
        uext_ref[e, HIST:HIST + CHUNK, 0:D_SSD] = xr_ref[e]
        uext_ref[e, HIST:HIST + CHUNK, D_SSD:D_XBC] = bcr_ref[e]
        hist_ref[e, :, 0:D_SSD] = xr_ref[e, CHUNK - HIST:CHUNK, :]
        hist_ref[e, :, D_SSD:D_XBC] = bcr_ref[e, CHUNK - HIST:CHUNK, :]
    for e in seqs:
        sh_ref[e] = jnp.dot(shift_mat, uext_ref[e], preferred_element_type=F32)
    for s in range(D_XBC // CONV_COLS):
        cs = slice(s * CONV_COLS, (s + 1) * CONV_COLS)
        for e in seqs:
            u = cb_ref[:, cs] + cw_ref[CONV_K - 1:CONV_K, cs] * uext_ref[e, HIST:HIST + CHUNK, cs].astype(F32)
            for kk in range(N_SHIFT):
                blk = N_SHIFT - 1 - kk
                u = u + cw_ref[kk:kk + 1, cs] * sh_ref[e, blk * CHUNK:(blk + 1) * CHUNK, cs]
            u = u * jax.nn.sigmoid(u)
            if s * CONV_COLS < D_SSD:
                xs_ref[e, :, cs] = u
            else:
                bc_ref[e, :, s * CONV_COLS - D_SSD:(s + 1) * CONV_COLS - D_SSD] = u.astype(BF16)

    lane = lax.broadcasted_iota(jnp.int32, (CHUNK, LANES), 1)
    is_dt = jnp.logical_and(lane >= DT_LANE0, lane < DT_LANE0 + N_SSD_HEADS)
    expand = e_ref[...]
    for e in seqs:
        acs = acol_ref[e]
        a_last = acs[CHUNK - 1:CHUNK, :]
        exp_a = jnp.where(is_dt, jnp.exp2(acs), 0.0)
        dec_end = jnp.where(is_dt, jnp.exp2(a_last - acs), 0.0)
        ex_ref[e, 0] = _split2_dot(dt_ref[e], expand)
        ex_ref[e, 1] = _split2_dot(exp_a, expand)
        ex_ref[e, 2] = _split2_dot(dec_end, expand)
    first_head = lax.broadcasted_iota(jnp.int32, (CHUNK, PAIR_W), 1) < SSD_HEAD_DIM
    for e in seqs:
        xdt = xs_ref[e] * ex_ref[e, 0]
        xdec_ref[e] = (xdt * ex_ref[e, 2]).astype(BF16)
        for pp in range(N_SSD_HEADS // 2):
            pair = xdt[:, pp * PAIR_W:(pp + 1) * PAIR_W]
            xbd_ref[e, pp, 0:CHUNK, :] = jnp.where(first_head, pair, 0.0).astype(BF16)
            xbd_ref[e, pp, CHUNK:2 * CHUNK, :] = jnp.where(first_head, 0.0, pair).astype(BF16)

    rr = lax.broadcasted_iota(jnp.int32, (CHUNK, CHUNK), 0)
    cc = lax.broadcasted_iota(jnp.int32, (CHUNK, CHUNK), 1)
    causal = rr >= cc
    for g in range(SSD_GROUPS):
        gs = slice(g * GROUP_W, (g + 1) * GROUP_W)
        for e in seqs:
            acs = acol_ref[e]
            arow = arow_ref[e]
            bg = bc_ref[e, :, g * D_STATE:(g + 1) * D_STATE]
            cg = bc_ref[e, :, D_BC + g * D_STATE:D_BC + (g + 1) * D_STATE]
            cb = lax.dot_general(cg, bg, NT_DIMS, preferred_element_type=F32)
            st = st_ref[e, g]
            y_off = jnp.dot(cg, st.astype(BF16), preferred_element_type=F32) * ex_ref[e, 1, :, gs]
            yacc_ref[e, :, gs] = y_off
            for pp in range(g * HEADS_PER_GROUP // 2, (g + 1) * HEADS_PER_GROUP // 2):
                mats = []
                for h in (2 * pp, 2 * pp + 1):
                    col = acs[:, DT_LANE0 + h:DT_LANE0 + h + 1]
                    row = arow[DT_LANE0 + h:DT_LANE0 + h + 1, :]
                    decay = jnp.exp2(jnp.where(causal, col - row, -jnp.inf))
                    mats.append((cb * decay).astype(BF16))
                psl = slice(pp * PAIR_W, (pp + 1) * PAIR_W)
                yacc_ref[e, :, psl] += jnp.dot(jnp.concatenate(mats, axis=1), xbd_ref[e, pp],
                                               preferred_element_type=F32)
            upd = lax.dot_general(bg, xdec_ref[e, :, gs], (((0,), (0,)), ((), ())), preferred_element_type=F32)
            st_ref[e, g] = st * ex_ref[e, 1, CHUNK - 1:CHUNK, gs] + upd

    for g in range(SSD_GROUPS):
        gs = slice(g * GROUP_W, (g + 1) * GROUP_W)
        for e in seqs:
            z = z_ref[e, :, gs].astype(F32)
            u = (yacc_ref[e, :, gs] + dsk_ref[:, gs] * xs_ref[e, :, gs]) * (z * jax.nn.sigmoid(z))
            ms = jnp.mean(u * u, axis=-1, keepdims=True)
            y_ref[e, :, gs] = (u * lax.rsqrt(ms + NORM_EPS) * nw_ref[:, gs]).astype(BF16)


def _ssd(proj3, conv_w, conv_b, acol, arow, dtv, expand, dskip_e, norm_w):
    b, seq, _ = proj3.shape
    nc = seq // CHUNK
    nb = SSD_BATCH
    zb = (3 * D_ATT + D_ATT) // D_SSD
    return pl.pallas_call(
        _ssd_kernel,
        grid=(b // nb, nc),
        in_specs=[
            pl.BlockSpec((nb, CHUNK, D_SSD), lambda bi, ci: (bi, ci, zb)),
            pl.BlockSpec((nb, CHUNK, D_SSD), lambda bi, ci: (bi, ci, zb + 1)),
            pl.BlockSpec((nb, CHUNK, D_SSD), lambda bi, ci: (bi, ci, zb + 2)),
            pl.BlockSpec((CONV_K, D_XBC), lambda bi, ci: (0, 0)),
            pl.BlockSpec((1, D_XBC), lambda bi, ci: (0, 0)),
            pl.BlockSpec((nb, CHUNK, LANES), lambda bi, ci: (bi, ci, 0)),
            pl.BlockSpec((nb, None, CHUNK, LANES), lambda bi, ci: (bi, ci, 0, 0)),
            pl.BlockSpec((nb, CHUNK, LANES), lambda bi, ci: (bi, ci, 0)),
            pl.BlockSpec((2 * LANES, D_SSD), lambda bi, ci: (0, 0)),
            pl.BlockSpec((1, D_SSD), lambda bi, ci: (0, 0)),
            pl.BlockSpec((1, D_SSD), lambda bi, ci: (0, 0)),
        ],
        out_specs=pl.BlockSpec((nb, CHUNK, D_SSD), lambda bi, ci: (bi, ci, 0)),
        out_shape=jax.ShapeDtypeStruct((b, seq, D_SSD), BF16),
        scratch_shapes=[
            pltpu.VMEM((nb, HIST + CHUNK, D_XBC), BF16),
            pltpu.VMEM((nb, HIST, D_XBC), BF16),
            pltpu.VMEM((nb, N_SHIFT * CHUNK, D_XBC), F32),
            pltpu.VMEM((nb, SSD_GROUPS, D_STATE, GROUP_W), F32),
            pltpu.VMEM((nb, CHUNK, D_SSD), F32),
            pltpu.VMEM((nb, CHUNK, 2 * D_BC), BF16),
            pltpu.VMEM((nb, 3, CHUNK, D_SSD), F32),
            pltpu.VMEM((nb, N_SSD_HEADS // 2, 2 * CHUNK, PAIR_W), BF16),
            pltpu.VMEM((nb, CHUNK, D_SSD), BF16),
            pltpu.VMEM((nb, CHUNK, D_SSD), F32),
        ],
        compiler_params=_cparams(("parallel", "arbitrary")),
        name="ssd",
    )(proj3, proj3, proj3, conv_w, conv_b, acol, arow, dtv, expand, dskip_e, norm_w)


TM_OUT = 512
OUT_SLABS = 2


def _out_kernel(att_ref, y_ref, w_ref, x_ref, g_ref, b_ref, o_ref, *, alpha):
    for s in range(OUT_SLABS):
        rows = slice(s * TM_OUT // OUT_SLABS, (s + 1) * TM_OUT // OUT_SLABS)
        h = jnp.dot(att_ref[rows, :], w_ref[0:D_ATT, :], preferred_element_type=F32)
        h = h + jnp.dot(y_ref[rows, :], w_ref[D_ATT:D_MIX, :], preferred_element_type=F32)
        r = alpha * x_ref[rows, :] + h
        mu = jnp.mean(r, axis=-1, keepdims=True)
        d = r - mu
        var = jnp.mean(d * d, axis=-1, keepdims=True)
        o_ref[rows, :] = d * lax.rsqrt(var + NORM_EPS) * g_ref[...] + b_ref[...]


def _outproj(att2, y2, w_out, x2, ln_g, ln_b, alpha):
    m = x2.shape[0]
    return pl.pallas_call(
        functools.partial(_out_kernel, alpha=alpha),
        grid=(m // TM_OUT,),
        in_specs=[
            pl.BlockSpec((TM_OUT, D_ATT), lambda i: (i, 0)),
            pl.BlockSpec((TM_OUT, D_SSD), lambda i: (i, 0)),
            pl.BlockSpec((D_MIX, D_MODEL), lambda i: (0, 0), pipeline_mode=pl.Buffered(1)),
            pl.BlockSpec((TM_OUT, D_MODEL), lambda i: (i, 0)),
            pl.BlockSpec((1, D_MODEL), lambda i: (0, 0)),
            pl.BlockSpec((1, D_MODEL), lambda i: (0, 0)),
        ],
        out_specs=pl.BlockSpec((TM_OUT, D_MODEL), lambda i: (i, 0)),
        out_shape=jax.ShapeDtypeStruct((m, D_MODEL), F32),
        compiler_params=_cparams(("parallel",)),
        name="outproj_deepnorm",
    )(att2, y2, w_out, x2, ln_g, ln_b)


def _pad_lanes(vec, lane0):
    out = jnp.zeros((1, LANES), F32)
    return out.at[0, lane0:lane0 + vec.shape[0]].set(vec.astype(F32))


def _layer(x, w_in, b_forget, conv_w, conv_b, dt_bias, a_log, d_skip, ssd_norm_w, w_out, ln_g, ln_b, alpha):
    b, seq, _ = x.shape
    x2 = x.reshape(b * seq, D_MODEL)

    w_main_t, w_small_t = _wprep(w_in.T)
    bias_vec = _pad_lanes(b_forget, F_LANE0) + _pad_lanes(dt_bias, DT_LANE0)
    alog_vec = _pad_lanes(a_log, DT_LANE0)
    heads = jnp.arange(D_SSD, dtype=jnp.int32) // SSD_HEAD_DIM
    expand = (jnp.arange(LANES, dtype=jnp.int32)[:, None] == heads[None, :] + DT_LANE0).astype(BF16)
    expand = jnp.concatenate([expand, expand], axis=0)
    dskip_e = jnp.repeat(d_skip.astype(F32), SSD_HEAD_DIM)[None, :]

    proj, small = _inproj(x2, w_main_t, w_small_t)
    proj3 = proj.reshape(b, seq, D_MAIN)
    gcol, dtv, grow, augq, augk = _gates(small.reshape(b, seq, LANES), bias_vec, alog_vec)

    att = _attention(proj3, augq, augk)
    y = _ssd(proj3, conv_w.astype(F32), conv_b.astype(F32)[None, :], gcol, grow, dtv, expand, dskip_e,
             ssd_norm_w.astype(F32)[None, :])
    out = _outproj(att.reshape(b * seq, D_ATT), y.reshape(b * seq, D_SSD), w_out.astype(BF16), x2,
                   ln_g.astype(F32)[None, :], ln_b.astype(F32)[None, :], alpha)
    return out.reshape(b, seq, D_MODEL)


def kernel(x, w_in, b_forget, conv_w, conv_b, dt_bias, a_log, d_skip, ssd_norm_w, w_out, ln_g, ln_b):
    depth = w_in.shape[0]
    alpha = (2.0 * depth) ** 0.25
    for i in range(depth):
        x = _layer(x, w_in[i], b_forget[i], conv_w[i], conv_b[i], dt_bias[i], a_log[i], d_skip[i],
                   ssd_norm_w[i], w_out[i], ln_g[i], ln_b[i], alpha)
    return x
```

```python
import functools
import math

import jax
import jax.numpy as jnp
from jax import lax
from jax.experimental import pallas as pl
from jax.experimental.pallas import tpu as pltpu

F32 = jnp.float32
BF16 = jnp.bfloat16

D_MODEL = 2048
D_ATT = 1024
HEAD_DIM = 128
N_ATT_HEADS = 8
D_SSD = 2048
SSD_HEAD_DIM = 64
N_SSD_HEADS = 32
SSD_GROUPS = 8
HEADS_PER_GROUP = N_SSD_HEADS // SSD_GROUPS
D_STATE = 128
CONV_K = 4
CHUNK = 128
D_BC = SSD_GROUPS * D_STATE
D_XBC = D_SSD + 2 * D_BC
D_MIX = D_ATT + D_SSD
NORM_EPS = 1e-5
LANES = 128
GROUP_W = HEADS_PER_GROUP * SSD_HEAD_DIM

_OFF_F = 3 * D_ATT
_OFF_ZATT = _OFF_F + N_ATT_HEADS
_OFF_DT = _OFF_ZATT + D_ATT + D_SSD + D_XBC
D_MAIN = 3 * D_ATT + D_ATT + D_SSD + D_XBC
F_LANE0 = 0
DT_LANE0 = N_ATT_HEADS

AUG_PIECES = 3
AUG_Q0 = AUG_PIECES * N_ATT_HEADS
LOG2E = math.log2(math.e)

VMEM_LIMIT = 56 * 1024 * 1024


def _cparams(sem, **kw):
    return pltpu.CompilerParams(dimension_semantics=sem, vmem_limit_bytes=VMEM_LIMIT, **kw)


TK_PREP = 256
ATT_SCALE = 1.0 / math.sqrt(HEAD_DIM)
NT_DIMS = (((1,), (1,)), ((), ()))


def _wprep_kernel(wt_ref, wm_ref, ws_ref):
    wm_ref[0:D_ATT, :] = (wt_ref[0:D_ATT, :] * (ATT_SCALE * LOG2E)).astype(BF16)
    wm_ref[D_ATT:_OFF_F, :] = wt_ref[D_ATT:_OFF_F, :].astype(BF16)
    wm_ref[_OFF_F:D_MAIN, :] = wt_ref[_OFF_ZATT:_OFF_DT, :].astype(BF16)
    small = jnp.concatenate(
        [wt_ref[_OFF_F:_OFF_ZATT, :], wt_ref[_OFF_DT:_OFF_DT + N_SSD_HEADS, :],
         jnp.zeros((LANES - N_ATT_HEADS - N_SSD_HEADS, TK_PREP), F32)], axis=0)
    ws_ref[...] = small.astype(BF16)


def _wprep(w_t):
    n, k = w_t.shape
    return pl.pallas_call(
        _wprep_kernel,
        grid=(k // TK_PREP,),
        in_specs=[pl.BlockSpec((n, TK_PREP), lambda i: (0, i))],
        out_specs=[pl.BlockSpec((D_MAIN, TK_PREP), lambda i: (0, i)),
                   pl.BlockSpec((LANES, TK_PREP), lambda i: (0, i))],
        out_shape=[jax.ShapeDtypeStruct((D_MAIN, k), BF16), jax.ShapeDtypeStruct((LANES, k), BF16)],
        compiler_params=_cparams(("parallel",)),
        name="weight_prep",
    )(w_t)


TM_IN = 1024
TN_IN = 2560


def _inproj_kernel(x_ref, w_ref, ws_ref, proj_ref, small_ref, xb_ref):
    @pl.when(pl.program_id(1) == 0)
    def _():
        xb = x_ref[...].astype(BF16)
        xb_ref[...] = xb
        small_ref[...] = lax.dot_general(xb, ws_ref[...], NT_DIMS, preferred_element_type=F32)

    proj_ref[...] = lax.dot_general(xb_ref[...], w_ref[...], NT_DIMS, preferred_element_type=F32).astype(BF16)


def _inproj(x2, w_main_t, w_small_t):
    m = x2.shape[0]
    return pl.pallas_call(
        _inproj_kernel,
        grid=(m // TM_IN, D_MAIN // TN_IN),
        in_specs=[
            pl.BlockSpec((TM_IN, D_MODEL), lambda i, j: (i, 0)),
            pl.BlockSpec((TN_IN, D_MODEL), lambda i, j: (j, 0)),
            pl.BlockSpec((LANES, D_MODEL), lambda i, j: (0, 0)),
        ],
        out_specs=[
            pl.BlockSpec((TM_IN, TN_IN), lambda i, j: (i, j)),
            pl.BlockSpec((TM_IN, LANES), lambda i, j: (i, 0)),
        ],
        out_shape=[
            jax.ShapeDtypeStruct((m, D_MAIN), BF16),
            jax.ShapeDtypeStruct((m, LANES), F32),
        ],
        scratch_shapes=[pltpu.VMEM((TM_IN, D_MODEL), BF16)],
        compiler_params=_cparams(("parallel", "arbitrary")),
        name="inproj",
    )(x2, w_main_t, w_small_t)


def _split_dot(t_bf16, val):
    hi = val.astype(BF16)
    r1 = val - hi.astype(F32)
    mid = r1.astype(BF16)
    lo = (r1 - mid.astype(F32)).astype(BF16)
    return (jnp.dot(t_bf16, hi, preferred_element_type=F32)
            + jnp.dot(t_bf16, mid, preferred_element_type=F32)
            + jnp.dot(t_bf16, lo, preferred_element_type=F32))


def _split3(val):
    hi = val.astype(BF16)
    r1 = val - hi.astype(F32)
    mid = r1.astype(BF16)
    lo = (r1 - mid.astype(F32)).astype(BF16)
    return hi, mid, lo


def _gate_kernel(small_ref, bias_ref, alog_ref, col_ref, dt_ref, row_ref, augq_ref, augk_ref):
    seq = small_ref.shape[0]
    lane = lax.broadcasted_iota(jnp.int32, (CHUNK, LANES), 1)
    is_f = lane < DT_LANE0
    is_dt = jnp.logical_and(lane >= DT_LANE0, lane < DT_LANE0 + N_SSD_HEADS)
    r = lax.broadcasted_iota(jnp.int32, (CHUNK, CHUNK), 0)
    c = lax.broadcasted_iota(jnp.int32, (CHUNK, CHUNK), 1)
    tri = jnp.where(r >= c, 1.0, 0.0).astype(BF16)
    is_head = r < N_ATT_HEADS
    place_k = [jnp.where(jnp.logical_and(is_head, c == AUG_PIECES * r + jj), 1.0, 0.0).astype(BF16)
               for jj in range(AUG_PIECES)]
    place_q = [jnp.where(jnp.logical_and(is_head, c == AUG_Q0 + AUG_PIECES * r + jj), 1.0, 0.0).astype(BF16)
               for jj in range(AUG_PIECES)]
    a_neg = -jnp.exp(alog_ref[...])
    bias = bias_ref[...]
    for ci in range(seq // CHUNK):
        rows = slice(ci * CHUNK, (ci + 1) * CHUNK)
        v = small_ref[rows, :] + bias
        t = jnp.log1p(jnp.exp(-jnp.abs(v)))
        log_f = jnp.minimum(v, 0.0) - t
        dt = jnp.maximum(v, 0.0) + t
        val = jnp.where(is_f, log_f, jnp.where(is_dt, dt * a_neg, 0.0)) * LOG2E
        col_ref[rows, :] = _split_dot(tri, val)
        dt_ref[rows, :] = jnp.where(is_dt, dt, 0.0)
    carry = jnp.zeros((1, LANES), F32)
    for ci in range(seq // CHUNK):
        rows = slice(ci * CHUNK, (ci + 1) * CHUNK)
        out = col_ref[rows, :] + carry
        col_ref[rows, :] = out
        row_ref[ci] = out.T
        pieces = _split3(out)
        acc_k = sum(jnp.dot(p, m, preferred_element_type=F32) for p, m in zip(pieces, place_k))
        acc_q = sum(jnp.dot(p, m, preferred_element_type=F32) for p, m in zip(pieces, place_q))
        in_k = lane < AUG_Q0
        in_q = jnp.logical_and(lane >= AUG_Q0, lane < 2 * AUG_Q0)
        augk_ref[rows, :] = jnp.where(in_k, -acc_k, jnp.where(in_q, 1.0, 0.0)).astype(BF16)
        augq_ref[rows, :] = jnp.where(in_k, 1.0, acc_q).astype(BF16)
        carry = jnp.where(is_f[:1], out[CHUNK - 1:CHUNK, :], 0.0)


def _gates(small3, bias_vec, alog_vec):
    b, seq, _ = small3.shape
    nc = seq // CHUNK
    return pl.pallas_call(
        _gate_kernel,
        grid=(b,),
        in_specs=[
            pl.BlockSpec((None, seq, LANES), lambda i: (i, 0, 0)),
            pl.BlockSpec((1, LANES), lambda i: (0, 0)),
            pl.BlockSpec((1, LANES), lambda i: (0, 0)),
        ],
        out_specs=[
            pl.BlockSpec((None, seq, LANES), lambda i: (i, 0, 0)),
            pl.BlockSpec((None, seq, LANES), lambda i: (i, 0, 0)),
            pl.BlockSpec((None, nc, CHUNK, LANES), lambda i: (i, 0, 0, 0)),
            pl.BlockSpec((None, seq, LANES), lambda i: (i, 0, 0)),
            pl.BlockSpec((None, seq, LANES), lambda i: (i, 0, 0)),
        ],
        out_shape=[
            jax.ShapeDtypeStruct((b, seq, LANES), F32),
            jax.ShapeDtypeStruct((b, seq, LANES), F32),
            jax.ShapeDtypeStruct((b, nc, CHUNK, LANES), F32),
            jax.ShapeDtypeStruct((b, seq, LANES), BF16),
            jax.ShapeDtypeStruct((b, seq, LANES), BF16),
        ],
        compiler_params=_cparams(("parallel",)),
        name="gates",
    )(small3, bias_vec, alog_vec)


TQ = 512
TK = 512
HEADS_PER_STEP = 2
BUFFER_SETS = 2
KX_W = HEAD_DIM + LANES


def _attn_kernel(q_ref, k_ref, v_ref, z_ref, aq_ref, ak_ref, o_ref,
                 s_ref, p_ref, acc_ref):
    i = pl.program_id(1)
    seq = k_ref.shape[0]

    def kx(h, keys):
        return jnp.concatenate([k_ref[keys, h * HEAD_DIM:(h + 1) * HEAD_DIM], ak_ref[keys, :]], axis=1)

    def vx(h, keys):
        ones = jnp.ones((keys.stop - keys.start, LANES), BF16)
        return jnp.concatenate([v_ref[keys, h * HEAD_DIM:(h + 1) * HEAD_DIM], ones], axis=1)

    lane = lax.broadcasted_iota(jnp.int32, (1, LANES), 1)

    def qx(h):
        k_lo = AUG_PIECES * h
        q_lo = AUG_Q0 + AUG_PIECES * h
        own = jnp.logical_or(jnp.logical_and(lane >= k_lo, lane < k_lo + AUG_PIECES),
                             jnp.logical_and(lane >= q_lo, lane < q_lo + AUG_PIECES))
        aug = (aq_ref[...].astype(F32) * jnp.where(own, 1.0, 0.0)).astype(BF16)
        return jnp.concatenate([q_ref[:, h * HEAD_DIM:(h + 1) * HEAD_DIM], aug], axis=1)

    top_rows, bot_rows = slice(0, TQ // 2), slice(TQ // 2, TQ)
    top_keys, bot_keys = slice(0, TK // 2), slice(0, TK)
    rt = lax.broadcasted_iota(jnp.int32, (TQ // 2, TK // 2), 0)
    ct = lax.broadcasted_iota(jnp.int32, (TQ // 2, TK // 2), 1)
    rb = lax.broadcasted_iota(jnp.int32, (TQ // 2, TK), 0)
    cb = lax.broadcasted_iota(jnp.int32, (TQ // 2, TK), 1)
    diag_parts = ((top_rows, top_keys, rt >= ct), (bot_rows, bot_keys, rb + TQ // 2 >= cb))

    def head_group(g, n_blocks):
        heads = [g * HEADS_PER_STEP + hh for hh in range(HEADS_PER_STEP)]
        slot = [(g % BUFFER_SETS) * HEADS_PER_STEP + hh for hh in range(HEADS_PER_STEP)]
        last = n_blocks - 1
        q_ext = [qx(h) for h in heads]

        def qk(hh, j):
            h = heads[hh]
            if j < last:
                s = lax.dot_general(q_ext[hh], kx(h, slice(j * TK, (j + 1) * TK)), NT_DIMS,
                                    preferred_element_type=F32)
                s_ref[slot[hh], j] = s
                return jnp.max(s, axis=1, keepdims=True)
            maxes = []
            for rows, keys, visible in diag_parts:
                s = lax.dot_general(q_ext[hh][rows, :], kx(h, slice(j * TK + keys.start, j * TK + keys.stop)),
                                    NT_DIMS, preferred_element_type=F32)
                s = jnp.where(visible, s, -jnp.inf)
                s_ref[slot[hh], j, rows, keys] = s
                maxes.append(jnp.max(s, axis=1, keepdims=True))
            return maxes

        m = [None] * HEADS_PER_STEP
        blk_max = [qk(hh, 0) for hh in range(HEADS_PER_STEP)]
        for j in range(n_blocks):
            nxt_max = [qk(hh, j + 1) for hh in range(HEADS_PER_STEP)] if j + 1 < n_blocks else None
            for hh in range(HEADS_PER_STEP):
                h, sl = heads[hh], slot[hh]
                if j < last:
                    m_new = blk_max[hh] if j == 0 else jnp.maximum(m[hh], blk_max[hh])
                    p_ref[sl, j] = jnp.exp2(s_ref[sl, j] - m_new).astype(BF16)
                    pv = jnp.dot(p_ref[sl, j], vx(h, slice(j * TK, (j + 1) * TK)), preferred_element_type=F32)
                    if j == 0:
                        acc_ref[sl] = pv
                    else:
                        acc_ref[sl] = jnp.exp2(m[hh] - m_new) * acc_ref[sl] + pv
                    m[hh] = m_new
                    continue
                for (rows, keys, _), part_max in zip(diag_parts, blk_max[hh]):
                    m_old = None if j == 0 else m[hh][rows]
                    m_new = part_max if j == 0 else jnp.maximum(m_old, part_max)
                    p_ref[sl, j, rows, keys] = jnp.exp2(s_ref[sl, j, rows, keys] - m_new).astype(BF16)
                    pv = jnp.dot(p_ref[sl, j, rows, keys], vx(h, slice(j * TK + keys.start, j * TK + keys.stop)),
                                 preferred_element_type=F32)
                    if j == 0:
                        acc_ref[sl, rows] = pv
                    else:
                        acc_ref[sl, rows] = jnp.exp2(m_old - m_new) * acc_ref[sl, rows] + pv
            blk_max = nxt_max
        for hh in range(HEADS_PER_STEP):
            hs = slice(heads[hh] * HEAD_DIM, (heads[hh] + 1) * HEAD_DIM)
            acc = acc_ref[slot[hh]]
            z = z_ref[:, hs].astype(F32)
            o_ref[:, hs] = ((acc[:, 0:HEAD_DIM] / acc[:, HEAD_DIM:KX_W]) * (z * jax.nn.sigmoid(z))).astype(BF16)

    for n_blocks in range(1, seq // TQ + 1):
        for g in range(N_ATT_HEADS // HEADS_PER_STEP):
            @pl.when(i == n_blocks - 1)
            def _(n_blocks=n_blocks, g=g):
                head_group(g, n_blocks)


def _attention(proj3, augq, augk):
    b, seq, _ = proj3.shape
    return pl.pallas_call(
        _attn_kernel,
        grid=(b, seq // TQ),
        in_specs=[
            pl.BlockSpec((None, TQ, D_ATT), lambda bi, i: (bi, i, 0)),
            pl.BlockSpec((None, seq, D_ATT), lambda bi, i: (bi, 0, 1)),
            pl.BlockSpec((None, seq, D_ATT), lambda bi, i: (bi, 0, 2)),
            pl.BlockSpec((None, TQ, D_ATT), lambda bi, i: (bi, i, 3)),
            pl.BlockSpec((None, TQ, LANES), lambda bi, i: (bi, i, 0)),
            pl.BlockSpec((None, seq, LANES), lambda bi, i: (bi, 0, 0)),
        ],
        out_specs=pl.BlockSpec((None, TQ, D_ATT), lambda bi, i: (bi, i, 0)),
        out_shape=jax.ShapeDtypeStruct((b, seq, D_ATT), BF16),
        scratch_shapes=[
            pltpu.VMEM((BUFFER_SETS * HEADS_PER_STEP, seq // TK, TQ, TK), F32),
            pltpu.VMEM((BUFFER_SETS * HEADS_PER_STEP, seq // TK, TQ, TK), BF16),
            pltpu.VMEM((BUFFER_SETS * HEADS_PER_STEP, TQ, KX_W), F32),
        ],
        compiler_params=_cparams(("parallel", "arbitrary")),
        name="fox_attention",
    )(proj3, proj3, proj3, proj3, augq, augk)


SSD_BATCH = 2
HIST = 16
N_SHIFT = CONV_K - 1
CONV_COLS = 512
PAIR_W = 2 * SSD_HEAD_DIM


def _split2_dot(val, e2_bf16):
    hi = val.astype(BF16)
    lo = (val - hi.astype(F32)).astype(BF16)
    return jnp.dot(jnp.concatenate([hi, lo], axis=1), e2_bf16, preferred_element_type=F32)


def _ssd_kernel(z_ref, xr_ref, bcr_ref, cw_ref, cb_ref, acol_ref, arow_ref, dt_ref, e_ref, dsk_ref, nw_ref,
                y_ref, uext_ref, hist_ref, sh_ref, st_ref, xs_ref, bc_ref, ex_ref, xbd_ref, xdec_ref, yacc_ref):
    c = pl.program_id(1)
    seqs = range(SSD_BATCH)

    @pl.when(c == 0)
    def _():
        hist_ref[...] = jnp.zeros_like(hist_ref)
        st_ref[...] = jnp.zeros_like(st_ref)

    out_row = lax.broadcasted_iota(jnp.int32, (N_SHIFT * CHUNK, HIST + CHUNK), 0)
    src_row = lax.broadcasted_iota(jnp.int32, (N_SHIFT * CHUNK, HIST + CHUNK), 1)
    shift = out_row // CHUNK + 1
    shift_mat = jnp.where(src_row == HIST + out_row % CHUNK - shift, 1.0, 0.0).astype(BF16)
    for e in seqs:
        uext_ref[e, 0:HIST, :] = hist_ref[e]
        uext_ref[e, HIST:HIST + CHUNK, 0:D_SSD] = xr_ref[e]
        uext_ref[e, HIST:HIST + CHUNK, D_SSD:D_XBC] = bcr_ref[e]
        hist_ref[e, :, 0:D_SSD] = xr_ref[e, CHUNK - HIST:CHUNK, :]
        hist_ref[e, :, D_SSD:D_XBC] = bcr_ref[e, CHUNK - HIST:CHUNK, :]
    for e in seqs:
        sh_ref[e] = jnp.dot(shift_mat, uext_ref[e], preferred_element_type=F32)
    for s in range(D_XBC // CONV_COLS):
        cs = slice(s * CONV_COLS, (s + 1) * CONV_COLS)
        for e in seqs:
            u = cb_ref[:, cs] + cw_ref[CONV_K - 1:CONV_K, cs] * uext_ref[e, HIST:HIST + CHUNK, cs].astype(F32)
            for kk in range(N_SHIFT):
                blk = N_SHIFT - 1 - kk
                u = u + cw_ref[kk:kk + 1, cs] * sh_ref[e, blk * CHUNK:(blk + 1) * CHUNK, cs]
            u = u * jax.nn.sigmoid(u)
            if s * CONV_COLS < D_SSD:
                xs_ref[e, :, cs] = u
            else:
                bc_ref[e, :, s * CONV_COLS - D_SSD:(s + 1) * CONV_COLS - D_SSD] = u.astype(BF16)

    lane = lax.broadcasted_iota(jnp.int32, (CHUNK, LANES), 1)
    is_dt = jnp.logical_and(lane >= DT_LANE0, lane < DT_LANE0 + N_SSD_HEADS)
    expand = e_ref[...]
    for e in seqs:
        acs = acol_ref[e]
        a_last = acs[CHUNK - 1:CHUNK, :]
        exp_a = jnp.where(is_dt, jnp.exp2(acs), 0.0)
        dec_end = jnp.where(is_dt, jnp.exp2(a_last - acs), 0.0)
        ex_ref[e, 0] = _split2_dot(dt_ref[e], expand)
        ex_ref[e, 1] = _split2_dot(exp_a, expand)
        ex_ref[e, 2] = _split2_dot(dec_end, expand)
    first_head = lax.broadcasted_iota(jnp.int32, (CHUNK, PAIR_W), 1) < SSD_HEAD_DIM
    for e in seqs:
        xdt = xs_ref[e] * ex_ref[e, 0]
        xdec_ref[e] = (xdt * ex_ref[e, 2]).astype(BF16)
        for pp in range(N_SSD_HEADS // 2):
            pair = xdt[:, pp * PAIR_W:(pp + 1) * PAIR_W]
            xbd_ref[e, pp, 0:CHUNK, :] = jnp.where(first_head, pair, 0.0).astype(BF16)
            xbd_ref[e, pp, CHUNK:2 * CHUNK, :] = jnp.where(first_head, 0.0, pair).astype(BF16)

    rr = lax.broadcasted_iota(jnp.int32, (CHUNK, CHUNK), 0)
    cc = lax.broadcasted_iota(jnp.int32, (CHUNK, CHUNK), 1)
    causal = rr >= cc
    for g in range(SSD_GROUPS):
        gs = slice(g * GROUP_W, (g + 1) * GROUP_W)
        for e in seqs:
            acs = acol_ref[e]
            arow = arow_ref[e]
            bg = bc_ref[e, :, g * D_STATE:(g + 1) * D_STATE]
            cg = bc_ref[e, :, D_BC + g * D_STATE:D_BC + (g + 1) * D_STATE]
            cb = lax.dot_general(cg, bg, NT_DIMS, preferred_element_type=F32)
            st = st_ref[e, g]
            y_off = jnp.dot(cg, st.astype(BF16), preferred_element_type=F32) * ex_ref[e, 1, :, gs]
            yacc_ref[e, :, gs] = y_off
            for pp in range(g * HEADS_PER_GROUP // 2, (g + 1) * HEADS_PER_GROUP // 2):
                mats = []
                for h in (2 * pp, 2 * pp + 1):
                    col = acs[:, DT_LANE0 + h:DT_LANE0 + h + 1]
                    row = arow[DT_LANE0 + h:DT_LANE0 + h + 1, :]
                    decay = jnp.exp2(jnp.where(causal, col - row, -jnp.inf))
                    mats.append((cb * decay).astype(BF16))
                psl = slice(pp * PAIR_W, (pp + 1) * PAIR_W)
                yacc_ref[e, :, psl] += jnp.dot(jnp.concatenate(mats, axis=1), xbd_ref[e, pp],
                                               preferred_element_type=F32)
            upd = lax.dot_general(bg, xdec_ref[e, :, gs], (((0,), (0,)), ((), ())), preferred_element_type=F32)
            st_ref[e, g] = st * ex_ref[e, 1, CHUNK - 1:CHUNK, gs] + upd

    for g in range(SSD_GROUPS):
        gs = slice(g * GROUP_W, (g + 1) * GROUP_W)
        for e in seqs:
            z = z_ref[e, :, gs].astype(F32)
            u = (yacc_ref[e, :, gs] + dsk_ref[:, gs] * xs_ref[e, :, gs]) * (z * jax.nn.sigmoid(z))
            ms = jnp.mean(u * u, axis=-1, keepdims=True)
            y_ref[e, :, gs] = (u * lax.rsqrt(ms + NORM_EPS) * nw_ref[:, gs]).astype(BF16)


def _ssd(proj3, conv_w, conv_b, acol, arow, dtv, expand, dskip_e, norm_w):
    b, seq, _ = proj3.shape
    nc = seq // CHUNK
    nb = SSD_BATCH
    zb = (3 * D_ATT + D_ATT) // D_SSD
    return pl.pallas_call(
        _ssd_kernel,
        grid=(b // nb, nc),
        in_specs=[
            pl.BlockSpec((nb, CHUNK, D_SSD), lambda bi, ci: (bi, ci, zb)),
            pl.BlockSpec((nb, CHUNK, D_SSD), lambda bi, ci: (bi, ci, zb + 1)),
            pl.BlockSpec((nb, CHUNK, D_SSD), lambda bi, ci: (bi, ci, zb + 2)),
            pl.BlockSpec((CONV_K, D_XBC), lambda bi, ci: (0, 0)),
            pl.BlockSpec((1, D_XBC), lambda bi, ci: (0, 0)),
            pl.BlockSpec((nb, CHUNK, LANES), lambda bi, ci: (bi, ci, 0)),
            pl.BlockSpec((nb, None, CHUNK, LANES), lambda bi, ci: (bi, ci, 0, 0)),
            pl.BlockSpec((nb, CHUNK, LANES), lambda bi, ci: (bi, ci, 0)),
            pl.BlockSpec((2 * LANES, D_SSD), lambda bi, ci: (0, 0)),
            pl.BlockSpec((1, D_SSD), lambda bi, ci: (0, 0)),
            pl.BlockSpec((1, D_SSD), lambda bi, ci: (0, 0)),
        ],
        out_specs=pl.BlockSpec((nb, CHUNK, D_SSD), lambda bi, ci: (bi, ci, 0)),
        out_shape=jax.ShapeDtypeStruct((b, seq, D_SSD), BF16),
        scratch_shapes=[
            pltpu.VMEM((nb, HIST + CHUNK, D_XBC), BF16),
            pltpu.VMEM((nb, HIST, D_XBC), BF16),
            pltpu.VMEM((nb, N_SHIFT * CHUNK, D_XBC), F32),
            pltpu.VMEM((nb, SSD_GROUPS, D_STATE, GROUP_W), F32),
            pltpu.VMEM((nb, CHUNK, D_SSD), F32),
            pltpu.VMEM((nb, CHUNK, 2 * D_BC), BF16),
            pltpu.VMEM((nb, 3, CHUNK, D_SSD), F32),
            pltpu.VMEM((nb, N_SSD_HEADS // 2, 2 * CHUNK, PAIR_W), BF16),
            pltpu.VMEM((nb, CHUNK, D_SSD), BF16),
            pltpu.VMEM((nb, CHUNK, D_SSD), F32),
        ],
        compiler_params=_cparams(("parallel", "arbitrary")),
        name="ssd",
    )(proj3, proj3, proj3, conv_w, conv_b, acol, arow, dtv, expand, dskip_e, norm_w)


TM_OUT = 512
OUT_SLABS = 2


def _out_kernel(att_ref, y_ref, w_ref, x_ref, g_ref, b_ref, o_ref, *, alpha):
    for s in range(OUT_SLABS):
        rows = slice(s * TM_OUT // OUT_SLABS, (s + 1) * TM_OUT // OUT_SLABS)
        h = jnp.dot(att_ref[rows, :], w_ref[0:D_ATT, :], preferred_element_type=F32)
        h = h + jnp.dot(y_ref[rows, :], w_ref[D_ATT:D_MIX, :], preferred_element_type=F32)
        r = alpha * x_ref[rows, :] + h
        mu = jnp.mean(r, axis=-1, keepdims=True)
        d = r - mu
        var = jnp.mean(d * d, axis=-1, keepdims=True)
        o_ref[rows, :] = d * lax.rsqrt(var + NORM_EPS) * g_ref[...] + b_ref[...]


def _outproj(att2, y2, w_out, x2, ln_g, ln_b, alpha):
    m = x2.shape[0]
    return pl.pallas_call(
        functools.partial(_out_kernel, alpha=alpha),
        grid=(m // TM_OUT,),
        in_specs=[
            pl.BlockSpec((TM_OUT, D_ATT), lambda i: (i, 0)),
            pl.BlockSpec((TM_OUT, D_SSD), lambda i: (i, 0)),
            pl.BlockSpec((D_MIX, D_MODEL), lambda i: (0, 0), pipeline_mode=pl.Buffered(1)),
            pl.BlockSpec((TM_OUT, D_MODEL), lambda i: (i, 0)),
            pl.BlockSpec((1, D_MODEL), lambda i: (0, 0)),
            pl.BlockSpec((1, D_MODEL), lambda i: (0, 0)),
        ],
        out_specs=pl.BlockSpec((TM_OUT, D_MODEL), lambda i: (i, 0)),
        out_shape=jax.ShapeDtypeStruct((m, D_MODEL), F32),
        compiler_params=_cparams(("parallel",)),
        name="outproj_deepnorm",
    )(att2, y2, w_out, x2, ln_g, ln_b)


def _pad_lanes(vec, lane0):
    out = jnp.zeros((1, LANES), F32)
    return out.at[0, lane0:lane0 + vec.shape[0]].set(vec.astype(F32))


def _layer(x, w_in, b_forget, conv_w, conv_b, dt_bias, a_log, d_skip, ssd_norm_w, w_out, ln_g, ln_b, alpha):
    b, seq, _ = x.shape
    x2 = x.reshape(b * seq, D_MODEL)

    w_main_t, w_small_t = _wprep(w_in.T)
    bias_vec = _pad_lanes(b_forget, F_LANE0) + _pad_lanes(dt_bias, DT_LANE0)
    alog_vec = _pad_lanes(a_log, DT_LANE0)
    heads = jnp.arange(D_SSD, dtype=jnp.int32) // SSD_HEAD_DIM
    expand = (jnp.arange(LANES, dtype=jnp.int32)[:, None] == heads[None, :] + DT_LANE0).astype(BF16)
    expand = jnp.concatenate([expand, expand], axis=0)
    dskip_e = jnp.repeat(d_skip.astype(F32), SSD_HEAD_DIM)[None, :]

    proj, small = _inproj(x2, w_main_t, w_small_t)
    proj3 = proj.reshape(b, seq, D_MAIN)
    gcol, dtv, grow, augq, augk = _gates(small.reshape(b, seq, LANES), bias_vec, alog_vec)

    att = _attention(proj3, augq, augk)
    y = _ssd(proj3, conv_w.astype(F32), conv_b.astype(F32)[None, :], gcol, grow, dtv, expand, dskip_e,
             ssd_norm_w.astype(F32)[None, :])
    out = _outproj(att.reshape(b * seq, D_ATT), y.reshape(b * seq, D_SSD), w_out.astype(BF16), x2,
                   ln_g.astype(F32)[None, :], ln_b.astype(F32)[None, :], alpha)
    return out.reshape(b, seq, D_MODEL)


def kernel(x, w_in, b_forget, conv_w, conv_b, dt_bias, a_log, d_skip, ssd_norm_w, w_out, ln_g, ln_b):
    depth = w_in.shape[0]
    alpha = (2.0 * depth) ** 0.25
    for i in range(depth):
        x = _layer(x, w_in[i], b_forget[i], conv_w[i], conv_b[i], dt_bias[i], a_log[i], d_skip[i],
                   ssd_norm_w[i], w_out[i], ln_g[i], ln_b[i], alpha)
    return x
```
